```python
import jax, jax.numpy as jnp
from jax import lax
import numpy as np

D_MODEL = 1024
BATCH = 8
SEQ = 2048
DEPTH = 1

N_META = 16
D_SSD = D_MODEL
D_CONF = D_MODEL
D_MIX = D_SSD + D_CONF
SSD_HEADDIM = 64
SSD_HEADS = D_SSD // SSD_HEADDIM
SSD_GROUPS = 4
SSD_HPG = SSD_HEADS // SSD_GROUPS
SSD_STATE = 128
SSD_CONV = 4
SSD_CHUNK = 128
D_XBC = D_SSD + 2 * SSD_GROUPS * SSD_STATE
CONF_KERNEL = 31
D_IN = D_SSD + D_XBC + SSD_HEADS + 2 * D_CONF
PEER_HEADS = 8
PEER_NKEYS = 128
PEER_EXPERTS = PEER_NKEYS * PEER_NKEYS
PEER_DKEY = 256
PEER_TOPK = 16
PEER_BLOCK = 256
EPS = 1e-5

kernel_name = "hymba_ssd_conformer_peer_block"


def rmsnorm(x, w):
    xf = x.astype(jnp.float32)
    y = xf * lax.rsqrt(jnp.mean(xf * xf, axis=-1, keepdims=True) + EPS)
    return (y * w.astype(jnp.float32)).astype(x.dtype)


def causal_depthwise_conv(x, w, b):
    k = w.shape[0]
    y = lax.conv_general_dilated(x, w[:, None, :].astype(x.dtype), window_strides=(1,),
                                 padding=[(k - 1, 0)],
                                 dimension_numbers=("NWC", "WIO", "NWC"),
                                 feature_group_count=x.shape[-1])
    return y + b.astype(x.dtype)


def segsum(a):
    t = a.shape[-1]
    cs = jnp.cumsum(a, axis=-1)
    diff = cs[..., :, None] - cs[..., None, :]
    mask = jnp.tril(jnp.ones((t, t), dtype=bool))
    return jnp.where(mask, diff, -jnp.inf)


def ssd_chunked(xh, dt, a_neg, bm, cm):
    b, lp = xh.shape[0], xh.shape[1]
    nc = lp // SSD_CHUNK
    xh = xh.reshape(b, nc, SSD_CHUNK, SSD_GROUPS, SSD_HPG, SSD_HEADDIM)
    dtc = dt.reshape(b, nc, SSD_CHUNK, SSD_GROUPS, SSD_HPG)
    xdt = xh * dtc[..., None]
    bm = bm.reshape(b, nc, SSD_CHUNK, SSD_GROUPS, SSD_STATE)
    cm = cm.reshape(b, nc, SSD_CHUNK, SSD_GROUPS, SSD_STATE)
    dta = jnp.moveaxis(dtc * a_neg.reshape(SSD_GROUPS, SSD_HPG), 2, -1)
    a_cs = jnp.cumsum(dta, axis=-1)
    lmat = jnp.exp(segsum(dta))
    cb = jnp.einsum("bclgn,bcsgn->bcgls", cm, bm)
    y_diag = jnp.einsum("bcgls,bcgrls,bcsgrp->bclgrp", cb, lmat, xdt)
    decay_states = jnp.exp(a_cs[..., -1:] - a_cs)
    states = jnp.einsum("bclgn,bcgrl,bclgrp->bcgrpn", bm, decay_states, xdt)
    chunk_decay = jnp.exp(a_cs[..., -1])

    def step(h, inp):
        s, d = inp
        return h * d[..., None, None] + s, h

    h0 = jnp.zeros((b, SSD_GROUPS, SSD_HPG, SSD_HEADDIM, SSD_STATE), states.dtype)
    _, prev = lax.scan(step, h0, (jnp.moveaxis(states, 1, 0), jnp.moveaxis(chunk_decay, 1, 0)))
    prev = jnp.moveaxis(prev, 0, 1)
    y_off = jnp.einsum("bclgn,bcgrpn,bcgrl->bclgrp", cm, prev, jnp.exp(a_cs))
    return (y_diag + y_off).reshape(b, lp, SSD_HEADS, SSD_HEADDIM)


def ssd_mixer(z, xbc, dt_raw, conv_w, conv_b, dt_bias, a_log, d_skip, norm_w):
    b, l = z.shape[0], z.shape[1]
    xbc = jax.nn.silu(causal_depthwise_conv(xbc, conv_w, conv_b))
    xs, bm, cm = jnp.split(xbc, [D_SSD, D_SSD + SSD_GROUPS * SSD_STATE], axis=-1)
    f32 = jnp.float32
    dt = jax.nn.softplus(dt_raw.astype(f32) + dt_bias.astype(f32))
    a_neg = -jnp.exp(a_log.astype(f32))
    xh = xs.astype(f32).reshape(b, l, SSD_HEADS, SSD_HEADDIM)
    bm = bm.astype(f32).reshape(b, l, SSD_GROUPS, SSD_STATE)
    cm = cm.astype(f32).reshape(b, l, SSD_GROUPS, SSD_STATE)
    pad = (-N_META) % SSD_CHUNK
    padw = lambda t: jnp.pad(t, [(0, 0), (pad, 0)] + [(0, 0)] * (t.ndim - 2))
    y = ssd_chunked(padw(xh), padw(dt), a_neg, padw(bm), padw(cm))[:, pad:]
    y = y + d_skip.astype(f32)[:, None] * xh
    y = y.reshape(b, l, D_SSD) * jax.nn.silu(z.astype(f32))
    yg = y.reshape(b, l, SSD_GROUPS, D_SSD // SSD_GROUPS)
    yg = yg * lax.rsqrt(jnp.mean(yg * yg, axis=-1, keepdims=True) + EPS)
    return (yg.reshape(b, l, D_SSD) * norm_w.astype(f32)).astype(z.dtype)


def conformer_conv(u, conv_w, conv_b, ln_g, ln_b):
    a, g = jnp.split(u, 2, axis=-1)
    h = causal_depthwise_conv(a * jax.nn.sigmoid(g), conv_w, conv_b)
    hf = h.astype(jnp.float32)
    mu = jnp.mean(hf, axis=-1, keepdims=True)
    var = jnp.mean(jnp.square(hf - mu), axis=-1, keepdims=True)
    hn = (hf - mu) * lax.rsqrt(var + EPS) * ln_g.astype(jnp.float32) + ln_b.astype(jnp.float32)
    return jax.nn.silu(hn).astype(u.dtype)


def peer_ffn(x, w_query, sub_keys_1, sub_keys_2, w_down, w_up):
    b, l, d = x.shape
    t = b * l
    xt = x.reshape(t, d)
    q = (xt @ w_query).reshape(t, PEER_HEADS, PEER_DKEY)
    q1, q2 = jnp.split(q, 2, axis=-1)
    s1 = jnp.einsum("thd,hkd->thk", q1, sub_keys_1).astype(jnp.float32)
    s2 = jnp.einsum("thd,hkd->thk", q2, sub_keys_2).astype(jnp.float32)
    v1, i1 = lax.top_k(s1, PEER_TOPK)
    v2, i2 = lax.top_k(s2, PEER_TOPK)
    cand = (v1[..., :, None] + v2[..., None, :]).reshape(t, PEER_HEADS, PEER_TOPK * PEER_TOPK)
    cand_idx = (i1[..., :, None] * PEER_NKEYS + i2[..., None, :]).reshape(t, PEER_HEADS, PEER_TOPK * PEER_TOPK)
    top_s, pos = lax.top_k(cand, PEER_TOPK)
    expert_idx = jnp.take_along_axis(cand_idx, pos, axis=-1)
    gate = jax.nn.softmax(top_s, axis=-1).astype(x.dtype)
    tp = -(-t // PEER_BLOCK) * PEER_BLOCK
    nb = tp // PEER_BLOCK
    xp = jnp.pad(xt, [(0, tp - t), (0, 0)]).reshape(nb, PEER_BLOCK, d)
    ip = jnp.pad(expert_idx, [(0, tp - t), (0, 0), (0, 0)]).reshape(nb, PEER_BLOCK, PEER_HEADS, PEER_TOPK)
    gp = jnp.pad(gate, [(0, tp - t), (0, 0), (0, 0)]).reshape(nb, PEER_BLOCK, PEER_HEADS, PEER_TOPK)

    def block(args):
        xb, ib, gb = args
        u = w_down[ib]
        act = jax.nn.gelu(jnp.einsum("td,thkd->thk", xb, u), approximate=False) * gb
        v = w_up[ib]
        return jnp.einsum("thk,thkd->td", act, v)

    y = lax.map(block, (xp, ip, gp))
    return y.reshape(tp, d)[:t].reshape(b, l, d)


def setup_inputs(seed: int = 0) -> dict:
    key = jax.random.key(seed)
    ks = jax.random.split(key, 24)
    nrm = lambda k, shape, s: jax.random.normal(k, shape, jnp.float32) * s
    L = DEPTH
    dt0 = jnp.exp(jax.random.uniform(ks[5], (L, SSD_HEADS), jnp.float32, np.log(1e-3), np.log(1e-1)))
    return {
        "x": nrm(ks[0], (BATCH, SEQ, D_MODEL), 1.0),
        "meta_tokens": nrm(ks[1], (N_META, D_MODEL), 1.0),
        "norm_mix_w": 1.0 + nrm(ks[2], (L, D_MODEL), 0.02),
        "w_in": nrm(ks[3], (L, D_MODEL, D_IN), D_MODEL ** -0.5),
        "ssd_conv_w": nrm(ks[4], (L, SSD_CONV, D_XBC), SSD_CONV ** -0.5),
        "ssd_conv_b": nrm(ks[6], (L, D_XBC), 0.01),
        "ssd_dt_bias": dt0 + jnp.log(-jnp.expm1(-dt0)),
        "ssd_A_log": jnp.log(jax.random.uniform(ks[7], (L, SSD_HEADS), jnp.float32, 1.0, 16.0)),
        "ssd_D": 1.0 + nrm(ks[8], (L, SSD_HEADS), 0.02),
        "ssd_norm_w": 1.0 + nrm(ks[9], (L, D_SSD), 0.02),
        "conf_conv_w": nrm(ks[10], (L, CONF_KERNEL, D_CONF), CONF_KERNEL ** -0.5),
        "conf_conv_b": nrm(ks[11], (L, D_CONF), 0.01),
        "conf_ln_g": 1.0 + nrm(ks[12], (L, D_CONF), 0.02),
        "conf_ln_b": nrm(ks[13], (L, D_CONF), 0.01),
        "w_out": nrm(ks[14], (L, D_MIX, D_MODEL), D_MIX ** -0.5),
        "norm_ffn_w": 1.0 + nrm(ks[15], (L, D_MODEL), 0.02),
        "peer_w_query": nrm(ks[16], (L, D_MODEL, PEER_HEADS * PEER_DKEY), D_MODEL ** -0.5),
        "peer_sub_keys_1": nrm(ks[17], (L, PEER_HEADS, PEER_NKEYS, PEER_DKEY // 2), (PEER_DKEY // 2) ** -0.5),
        "peer_sub_keys_2": nrm(ks[18], (L, PEER_HEADS, PEER_NKEYS, PEER_DKEY // 2), (PEER_DKEY // 2) ** -0.5),
        "peer_w_down": nrm(ks[19], (L, PEER_EXPERTS, D_MODEL), D_MODEL ** -0.5),
        "peer_w_up": nrm(ks[20], (L, PEER_EXPERTS, D_MODEL), 0.25),
        "norm_final_w": 1.0 + nrm(ks[21], (D_MODEL,), 0.02),
    }


def reference(x, meta_tokens, norm_mix_w, w_in, ssd_conv_w, ssd_conv_b, ssd_dt_bias, ssd_A_log, ssd_D,
              ssd_norm_w, conf_conv_w, conf_conv_b, conf_ln_g, conf_ln_b, w_out, norm_ffn_w,
              peer_w_query, peer_sub_keys_1, peer_sub_keys_2, peer_w_down, peer_w_up, norm_final_w):
    b = x.shape[0]
    meta = jnp.broadcast_to(meta_tokens.astype(x.dtype)[None], (b, N_META, D_MODEL))
    h = jnp.concatenate([meta, x], axis=1)
    for l in range(DEPTH):
        u = rmsnorm(h, norm_mix_w[l])
        proj = u @ w_in[l]
        z, xbc, dt_raw, conf_in = jnp.split(
            proj, [D_SSD, D_SSD + D_XBC, D_SSD + D_XBC + SSD_HEADS], axis=-1)
        y_ssd = ssd_mixer(z, xbc, dt_raw, ssd_conv_w[l], ssd_conv_b[l], ssd_dt_bias[l],
                          ssd_A_log[l], ssd_D[l], ssd_norm_w[l])
        y_conf = conformer_conv(conf_in, conf_conv_w[l], conf_conv_b[l], conf_ln_g[l], conf_ln_b[l])
        h = h + jnp.concatenate([y_ssd, y_conf], axis=-1) @ w_out[l]
        h = h + peer_ffn(rmsnorm(h, norm_ffn_w[l]), peer_w_query[l], peer_sub_keys_1[l],
                         peer_sub_keys_2[l], peer_w_down[l], peer_w_up[l])
    return rmsnorm(h, norm_final_w)[:, N_META:]
```

```python
import functools
import math

import jax
import jax.numpy as jnp
from jax import lax
from jax.experimental import pallas as pl
from jax.experimental.pallas import tpu as pltpu

F32 = jnp.float32
BF16 = jnp.bfloat16

EPS = 1e-5
N_META = 16
SSD_HEADDIM = 64
SSD_GROUPS = 4
SSD_STATE = 128
SSD_CONV = 4
CONF_KERNEL = 31
PEER_HEADS = 8
PEER_NKEYS = 128
PEER_TOPK = 16

LANES = 128
SUBLANES = 8
VMEM_LIMIT_BYTES = 56 * 1024 * 1024

SSD_SUB = 128
XBC_HALO = 8
GLU_HALO = 32
DT_PAD = LANES

MIX_TILE = 256
PEER_TOK = 512
PEER_EXP = SUBLANES * PEER_NKEYS


def _dot(a, b):
    return jnp.dot(a, b, preferred_element_type=F32)


def _dot_nt(a, b):
    return lax.dot_general(a, b, (((1,), (1,)), ((), ())), preferred_element_type=F32)


def _dot_tn(a, b):
    return lax.dot_general(a, b, (((0,), (0,)), ((), ())), preferred_element_type=F32)


def _rmsnorm(x, w):
    return x * lax.rsqrt(jnp.mean(x * x, axis=-1, keepdims=True) + EPS) * w


def _silu(x):
    return x * jax.nn.sigmoid(x)


def _gelu(x):
    return 0.5 * x * (1.0 + lax.erf(x * math.sqrt(0.5)))


def _softplus(x):
    return jnp.maximum(x, 0.0) + jnp.log1p(jnp.exp(-jnp.abs(x)))


def _split3(x):
    hi = x.astype(BF16)
    r1 = x - hi.astype(F32)
    mid = r1.astype(BF16)
    lo = (r1 - mid.astype(F32)).astype(BF16)
    return hi, mid, lo


def _expand_heads(a, n_heads):
    q = a.shape[0]
    lane = lax.broadcasted_iota(jnp.int32, (q, LANES), 1)
    blocks = []
    for j in range(n_heads // 2):
        blocks.append(jnp.where(lane < SSD_HEADDIM, a[:, 2 * j:2 * j + 1], a[:, 2 * j + 1:2 * j + 2]))
    return jnp.concatenate(blocks, axis=1)


def _mixer_tile(x, dt_keep, w, state_ref, xbc_ext_ref, glu_ext_ref, *, with_output):
    L, d_model = x.shape
    d_ssd = d_model
    n_heads = d_ssd // SSD_HEADDIM
    d_bc = SSD_GROUPS * SSD_STATE
    d_xbc = d_ssd + 2 * d_bc
    hpg = n_heads // SSD_GROUPS
    gw = hpg * SSD_HEADDIM

    u = _rmsnorm(x, w["norm_mix_w"][...]).astype(BF16)
    o_xbc = d_ssd
    o_conf = o_xbc + d_xbc
    o_dt = o_conf + 2 * d_model
    w_in = w["w_in"]
    xbc_pre = _dot(u, w_in[:, o_xbc:o_conf])
    dt_raw = _dot(u, w_in[:, o_dt:o_dt + DT_PAD])

    xbc_ext_ref[XBC_HALO:XBC_HALO + L, :] = xbc_pre
    acc = jnp.broadcast_to(w["ssd_conv_b"][...], (L, d_xbc))
    for k in range(SSD_CONV):
        off = XBC_HALO - (SSD_CONV - 1) + k
        acc = acc + xbc_ext_ref[pl.ds(off, L), :] * w["ssd_conv_w"][k:k + 1, :]
    xbc_ext_ref[0:XBC_HALO, :] = xbc_ext_ref[L:L + XBC_HALO, :]
    xbc = _silu(acc)

    dt = _softplus(dt_raw + w["ssd_dt_bias"][...])
    if dt_keep is not None:
        dt = dt * dt_keep
    a_neg = -jnp.exp(w["ssd_a_log"][...])
    dta = dt * a_neg

    row = lax.broadcasted_iota(jnp.int32, (SSD_SUB, SSD_SUB), 0)
    col = lax.broadcasted_iota(jnp.int32, (SSD_SUB, SSD_SUB), 1)
    causal = row >= col
    tri = jnp.where(causal, 1.0, 0.0).astype(BF16)
    lane = lax.broadcasted_iota(jnp.int32, (SSD_SUB, LANES), 1)

    y_rows = []
    for j in range(L // SSD_SUB):
        r0 = j * SSD_SUB
        xs = xbc[r0:r0 + SSD_SUB, 0:d_ssd]
        bm = xbc[r0:r0 + SSD_SUB, d_ssd:d_ssd + d_bc].astype(BF16)
        cm = xbc[r0:r0 + SSD_SUB, d_ssd + d_bc:d_xbc].astype(BF16)
        dt_j = dt[r0:r0 + SSD_SUB]
        hi, mid, lo = _split3(dta[r0:r0 + SSD_SUB])
        a_cs = _dot(tri, hi) + _dot(tri, mid) + _dot(tri, lo)
        a_exp = _expand_heads(a_cs, n_heads)
        dt_exp = _expand_heads(dt_j, n_heads)
        a_last = a_exp[SSD_SUB - 1:SSD_SUB, :]
        xdt = xs * dt_exp
        xds = (xdt * jnp.exp(a_last - a_exp)).astype(BF16)
        chunk_decay = jnp.exp(a_last)
        if with_output:
            a_cs_t = a_cs.T
            ea = jnp.exp(a_exp)
            xdt_b = xdt.astype(BF16)
        y_cols = []
        for g in range(SSD_GROUPS):
            bm_g = bm[:, g * SSD_STATE:(g + 1) * SSD_STATE]
            cm_g = cm[:, g * SSD_STATE:(g + 1) * SSD_STATE]
            st_prev = state_ref[g]
            new_st = _dot_tn(bm_g, xds[:, g * gw:(g + 1) * gw])
            state_ref[g] = st_prev * chunk_decay[:, g * gw:(g + 1) * gw] + new_st
            if not with_output:
                continue
            y_off = _dot(cm_g, st_prev.astype(BF16)) * ea[:, g * gw:(g + 1) * gw]
            cb = _dot_nt(cm_g, bm_g)
            for pr in range(hpg // 2):
                ys = []
                for hh in range(2):
                    hd = g * hpg + 2 * pr + hh
                    seg = a_cs[:, hd:hd + 1] - a_cs_t[hd:hd + 1, :]
                    lmat = jnp.where(causal, jnp.exp(jnp.minimum(seg, 0.0)), 0.0)
                    xp = xdt_b[:, (g * hpg + 2 * pr) * SSD_HEADDIM:(g * hpg + 2 * pr + 2) * SSD_HEADDIM]
                    ys.append(_dot((cb * lmat).astype(BF16), xp))
                yd = jnp.where(lane < SSD_HEADDIM, ys[0], ys[1])
                y_cols.append(yd + y_off[:, pr * LANES:(pr + 1) * LANES])
        if with_output:
            y_rows.append(jnp.concatenate(y_cols, axis=1) + w["ssd_d"][...] * xs)

    conf = _dot(u, w_in[:, o_conf:o_dt])
    glu = conf[:, :d_model] * jax.nn.sigmoid(conf[:, d_model:])
    glu_ext_ref[GLU_HALO:GLU_HALO + L, :] = glu
    if not with_output:
        glu_ext_ref[0:GLU_HALO, :] = glu_ext_ref[L:L + GLU_HALO, :]
        return None

    z = _dot(u, w_in[:, 0:d_ssd])
    y = jnp.concatenate(y_rows, axis=0) if len(y_rows) > 1 else y_rows[0]
    y = y * _silu(z)
    parts = []
    for g in range(SSD_GROUPS):
        yg = y[:, g * gw:(g + 1) * gw]
        parts.append(yg * lax.rsqrt(jnp.mean(yg * yg, axis=-1, keepdims=True) + EPS))
    y_ssd = jnp.concatenate(parts, axis=1) * w["ssd_norm_w"][...]

    hc = jnp.broadcast_to(w["conf_conv_b"][...], (L, d_model))
    for k in range(CONF_KERNEL):
        off = GLU_HALO - (CONF_KERNEL - 1) + k
        hc = hc + glu_ext_ref[pl.ds(off, L), :] * w["conf_conv_w"][k:k + 1, :]
    glu_ext_ref[0:GLU_HALO, :] = glu_ext_ref[L:L + GLU_HALO, :]
    mu = jnp.mean(hc, axis=-1, keepdims=True)
    hcc = hc - mu
    var = jnp.mean(hcc * hcc, axis=-1, keepdims=True)
    hn = hcc * lax.rsqrt(var + EPS) * w["conf_ln_g"][...] + w["conf_ln_b"][...]
    y_conf = _silu(hn)

    mix = jnp.concatenate([y_ssd, y_conf], axis=1).astype(BF16)
    return x + _dot(mix, w["w_out"][...])


_MIX_WEIGHTS = ("norm_mix_w", "w_in", "ssd_conv_w", "ssd_conv_b", "ssd_dt_bias", "ssd_a_log",
                "ssd_d", "ssd_norm_w", "conf_conv_w", "conf_conv_b", "conf_ln_g", "conf_ln_b", "w_out")


def _meta_kernel(x_ref, keep_ref, *refs):
    nw = len(_MIX_WEIGHTS)
    w = dict(zip(_MIX_WEIGHTS, refs[:nw]))
    state_ref, xbc_halo_ref, glu_halo_ref, xbc_ext_ref, glu_ext_ref = refs[nw:]
    state_ref[...] = jnp.zeros(state_ref.shape, F32)
    xbc_ext_ref[0:XBC_HALO, :] = jnp.zeros((XBC_HALO, xbc_ext_ref.shape[1]), F32)
    glu_ext_ref[0:GLU_HALO, :] = jnp.zeros((GLU_HALO, glu_ext_ref.shape[1]), F32)
    _mixer_tile(x_ref[...], keep_ref[...], w, state_ref, xbc_ext_ref, glu_ext_ref, with_output=False)
    xbc_halo_ref[...] = xbc_ext_ref[0:XBC_HALO, :]
    glu_halo_ref[...] = glu_ext_ref[0:GLU_HALO, :]


def _mixer_kernel(x_ref, state0_ref, xbc_halo0_ref, glu_halo0_ref, *refs):
    nw = len(_MIX_WEIGHTS)
    w = dict(zip(_MIX_WEIGHTS, refs[:nw]))
    out_ref, state_ref, xbc_ext_ref, glu_ext_ref = refs[nw:]

    @pl.when(pl.program_id(1) == 0)
    def _():
        state_ref[...] = state0_ref[...]
        xbc_ext_ref[0:XBC_HALO, :] = xbc_halo0_ref[...]
        glu_ext_ref[0:GLU_HALO, :] = glu_halo0_ref[...]

    out_ref[...] = _mixer_tile(x_ref[...], None, w, state_ref, xbc_ext_ref, glu_ext_ref,
                               with_output=True)


def _const_spec(shape):
    nd = len(shape)
    return pl.BlockSpec(shape, lambda *_: (0,) * nd, pipeline_mode=pl.Buffered(1))


def _mixer_calls(x, meta_tile, keep, wts):
    b, seq, d = x.shape
    n_heads = d // SSD_HEADDIM
    gw = (n_heads // SSD_GROUPS) * SSD_HEADDIM
    d_xbc = d + 2 * SSD_GROUPS * SSD_STATE
    w_list = [wts[k] for k in _MIX_WEIGHTS]
    state_shape = (SSD_GROUPS, SSD_STATE, gw)

    lm = meta_tile.shape[0]
    state0, xbc_halo0, glu_halo0 = pl.pallas_call(
        _meta_kernel,
        out_shape=(jax.ShapeDtypeStruct(state_shape, F32),
                   jax.ShapeDtypeStruct((XBC_HALO, d_xbc), F32),
                   jax.ShapeDtypeStruct((GLU_HALO, d), F32)),
        scratch_shapes=[pltpu.VMEM((XBC_HALO + lm, d_xbc), F32),
                        pltpu.VMEM((GLU_HALO + lm, d), F32)],
        compiler_params=pltpu.CompilerParams(vmem_limit_bytes=VMEM_LIMIT_BYTES),
        name="meta_prologue",
    )(meta_tile, keep, *w_list)

    lt = MIX_TILE
    assert seq % lt == 0
    return pl.pallas_call(
        _mixer_kernel,
        grid=(b, seq // lt),
        in_specs=[pl.BlockSpec((None, lt, d), lambda i, c: (i, c, 0)),
                  _const_spec(state_shape), _const_spec((XBC_HALO, d_xbc)), _const_spec((GLU_HALO, d))]
                 + [_const_spec(a.shape) for a in w_list],
        out_specs=pl.BlockSpec((None, lt, d), lambda i, c: (i, c, 0)),
        out_shape=jax.ShapeDtypeStruct((b, seq, d), F32),
        scratch_shapes=[pltpu.VMEM(state_shape, F32),
                        pltpu.VMEM((XBC_HALO + lt, d_xbc), F32),
                        pltpu.VMEM((GLU_HALO + lt, d), F32)],
        compiler_params=pltpu.CompilerParams(
            dimension_semantics=("arbitrary", "arbitrary"), vmem_limit_bytes=VMEM_LIMIT_BYTES),
        name="mixer",
    )(x, state0, xbc_halo0, glu_halo0, *w_list)


def _sort_desc(v):
    v = list(v)
    n = len(v)
    k = 2
    while k <= n:
        j = k // 2
        while j >= 1:
            for i in range(n):
                l = i ^ j
                if l > i:
                    hi, lo = jnp.maximum(v[i], v[l]), jnp.minimum(v[i], v[l])
                    v[i], v[l] = (hi, lo) if (i & k) == 0 else (lo, hi)
            j //= 2
        k *= 2
    return v


def _merge_top(a, b, *, sort=True):
    n = len(a)
    v = [jnp.maximum(a[i], b[n - 1 - i]) for i in range(n)]
    if not sort:
        return v
    j = n // 2
    while j >= 1:
        for i in range(n):
            l = i ^ j
            if l > i:
                v[i], v[l] = jnp.maximum(v[i], v[l]), jnp.minimum(v[i], v[l])
        j //= 2
    return v


def _top_sorted(s):
    n = s.shape[0] // SUBLANES
    v = _sort_desc([s[SUBLANES * i:SUBLANES * (i + 1), :] for i in range(n)])
    for shift in (4, 2, 1):
        v = _merge_top(v, [pltpu.roll(a, shift, axis=0) for a in v])
    return v


def _selection_stats(v1, v2):
    t = v1[0].shape[1]
    sub = lax.broadcasted_iota(jnp.int32, (SUBLANES, t), 0)
    lists = []
    for q in range(PEER_TOPK // SUBLANES):
        col = v1[SUBLANES * q]
        for s in range(1, SUBLANES):
            col = jnp.where(sub == s, v1[SUBLANES * q + s], col)
        lists.append([col + b for b in v2])
    m = _merge_top(lists[0], lists[1])
    m = _merge_top(m, [pltpu.roll(a, 4, axis=0) for a in m])
    m = _merge_top(m, [pltpu.roll(a, 2, axis=0) for a in m])
    m = _merge_top(m, [pltpu.roll(a, 1, axis=0) for a in m], sort=False)
    top = v1[0] + v2[0]
    tau = functools.reduce(jnp.minimum, m)
    z = functools.reduce(lambda a, b: a + b, [jnp.exp(c - top) for c in m])
    return tau, z


def _peer_kernel(h_ref, nfw_ref, wq_ref, k1_ref, k2_ref, wd_ref, wu_ref, nlw_ref, out_ref,
                 xt_ref, acc_ref, a_ref, s1_ref, e1_ref, s2_ref, e2_ref, tau_ref):
    e = pl.program_id(1)
    tt = h_ref.shape[0]
    ec = wd_ref.shape[0]
    dk = k1_ref.shape[2]

    @pl.when(e == 0)
    def _():
        xn = _rmsnorm(h_ref[...], nfw_ref[...])
        xt_ref[...] = xn.T.astype(BF16)
        qt = _dot(wq_ref[...], xt_ref[...]).astype(BF16)
        for h in range(PEER_HEADS):
            s1 = _dot(k1_ref[h], qt[2 * dk * h:2 * dk * h + dk, :])
            s2 = _dot(k2_ref[h], qt[2 * dk * h + dk:2 * dk * (h + 1), :])
            s2_ref[h] = s2
            for lt in range(tt // LANES):
                ls = slice(lt * LANES, (lt + 1) * LANES)
                v1 = _top_sorted(s1[:, ls])
                v2 = _top_sorted(s2[:, ls])
                tau, z = _selection_stats(v1, v2)
                tau_ref[h, :, ls] = tau
                e1 = jnp.exp(s1[:, ls] - v1[0][0:1, :])
                for blk in range(PEER_NKEYS // SUBLANES):
                    rows = slice(blk * SUBLANES, (blk + 1) * SUBLANES)
                    s1_ref[h, blk, :, ls] = s1[rows, ls]
                    e1_ref[h, blk, :, ls] = e1[rows, :]
                e2_ref[h, :, ls] = jnp.exp(s2[:, ls] - v2[0][0:1, :]) / z[0:1, :]
        acc_ref[...] = jnp.zeros(acc_ref.shape, F32)

    s = _dot(wd_ref[...], xt_ref[...])
    assert ec == SUBLANES * PEER_NKEYS
    for lt in range(tt // LANES):
        ls = pl.ds(lt * LANES, LANES)
        tau = [tau_ref[h, :, ls] for h in range(PEER_HEADS)]
        for k in range(SUBLANES):
            bshape = (SUBLANES, LANES)
            s1b = [jnp.broadcast_to(s1_ref[h, e, k:k + 1, ls], bshape) for h in range(PEER_HEADS)]
            e1b = [jnp.broadcast_to(e1_ref[h, e, k:k + 1, ls], bshape) for h in range(PEER_HEADS)]
            for bb in range(PEER_NKEYS // (2 * SUBLANES)):
                halves = []
                for half in range(2):
                    rs = pl.ds((2 * bb + half) * SUBLANES, SUBLANES)
                    g = None
                    for h in range(PEER_HEADS):
                        term = jnp.where(s1b[h] + s2_ref[h, rs, ls] >= tau[h], e2_ref[h, rs, ls], 0.0) * e1b[h]
                        g = term if g is None else g + term
                    halves.append(g)
                r0 = k * PEER_NKEYS + 2 * bb * SUBLANES
                gate = jnp.concatenate(halves, axis=0)
                pre = s[r0:r0 + 2 * SUBLANES, lt * LANES:(lt + 1) * LANES]
                a_ref[r0:r0 + 2 * SUBLANES, ls] = (_gelu(pre) * gate).astype(BF16)
    acc_ref[...] += _dot(wu_ref[...], a_ref[...])

    @pl.when(e == pl.num_programs(1) - 1)
    def _():
        out_ref[...] = _rmsnorm(h_ref[...] + acc_ref[...].T, nlw_ref[...])


def _peer_call(h, nfw, wq_t, k1, k2, wd, wu_t, nlw):
    t, d = h.shape
    n_exp = wd.shape[0]
    tt, ec = PEER_TOK, PEER_EXP
    assert t % tt == 0 and n_exp % ec == 0
    gate_shape = (PEER_HEADS, PEER_NKEYS, tt)
    row_shape = (PEER_HEADS, PEER_NKEYS // SUBLANES, SUBLANES, tt)
    return pl.pallas_call(
        _peer_kernel,
        grid=(t // tt, n_exp // ec),
        in_specs=[pl.BlockSpec((tt, d), lambda i, e: (i, 0)),
                  _const_spec(nfw.shape), _const_spec(wq_t.shape), _const_spec(k1.shape),
                  _const_spec(k2.shape),
                  pl.BlockSpec((ec, d), lambda i, e: (e, 0)),
                  pl.BlockSpec((d, ec), lambda i, e: (0, e)),
                  _const_spec(nlw.shape)],
        out_specs=pl.BlockSpec((tt, d), lambda i, e: (i, 0)),
        out_shape=jax.ShapeDtypeStruct((t, d), F32),
        scratch_shapes=[pltpu.VMEM((d, tt), BF16),
                        pltpu.VMEM((d, tt), F32),
                        pltpu.VMEM((ec, tt), BF16),
                        pltpu.VMEM(row_shape, F32), pltpu.VMEM(row_shape, F32),
                        pltpu.VMEM(gate_shape, F32), pltpu.VMEM(gate_shape, F32),
                        pltpu.VMEM((PEER_HEADS, SUBLANES, tt), F32)],
        compiler_params=pltpu.CompilerParams(
            dimension_semantics=("arbitrary", "arbitrary"), vmem_limit_bytes=VMEM_LIMIT_BYTES),
        name="peer",
    )(h, nfw, wq_t, k1, k2, wd, wu_t, nlw)


def _pad_lanes(a, width):
    return jnp.pad(a, [(0, 0)] * (a.ndim - 1) + [(0, width - a.shape[-1])])


def kernel(x, meta_tokens, norm_mix_w, w_in, ssd_conv_w, ssd_conv_b, ssd_dt_bias, ssd_A_log, ssd_D,
           ssd_norm_w, conf_conv_w, conf_conv_b, conf_ln_g, conf_ln_b, w_out, norm_ffn_w,
           peer_w_query, peer_sub_keys_1, peer_sub_keys_2, peer_w_down, peer_w_up, norm_final_w):
    b, seq, d = x.shape
    assert norm_mix_w.shape[0] == 1, "one layer"
    d_xbc = d + 2 * SSD_GROUPS * SSD_STATE
    n_heads = d // SSD_HEADDIM
    row = lambda a: a.reshape(1, -1).astype(F32)

    wi = w_in[0]
    o1, o2, o3 = d, d + d_xbc, d + d_xbc + n_heads
    w_in_r = jnp.concatenate([wi[:, :o1], wi[:, o1:o2], wi[:, o3:], _pad_lanes(wi[:, o2:o3], DT_PAD)],
                             axis=1).astype(BF16)
    wts = {
        "norm_mix_w": row(norm_mix_w[0]), "w_in": w_in_r,
        "ssd_conv_w": ssd_conv_w[0].astype(F32), "ssd_conv_b": row(ssd_conv_b[0]),
        "ssd_dt_bias": _pad_lanes(row(ssd_dt_bias[0]), DT_PAD),
        "ssd_a_log": _pad_lanes(row(ssd_A_log[0]), DT_PAD),
        "ssd_d": row(jnp.repeat(ssd_D[0], SSD_HEADDIM)), "ssd_norm_w": row(ssd_norm_w[0]),
        "conf_conv_w": conf_conv_w[0].astype(F32), "conf_conv_b": row(conf_conv_b[0]),
        "conf_ln_g": row(conf_ln_g[0]), "conf_ln_b": row(conf_ln_b[0]),
        "w_out": w_out[0].astype(BF16),
    }

    pad = SSD_SUB - N_META
    meta_tile = jnp.pad(meta_tokens.astype(F32), [(pad, 0), (0, 0)])
    keep = jnp.pad(jnp.ones((N_META, DT_PAD), F32), [(pad, 0), (0, 0)])
    h1 = _mixer_calls(x.astype(F32), meta_tile, keep, wts)

    out = _peer_call(
        h1.reshape(b * seq, d), row(norm_ffn_w[0]),
        peer_w_query[0].T.astype(BF16), peer_sub_keys_1[0].astype(BF16), peer_sub_keys_2[0].astype(BF16),
        peer_w_down[0].astype(BF16), peer_w_up[0].T.astype(BF16), row(norm_final_w))
    return out.reshape(b, seq, d).astype(x.dtype)
```

```python
import functools
import math

import jax
import jax.numpy as jnp
from jax import lax
from jax.experimental import pallas as pl
from jax.experimental.pallas import tpu as pltpu

F32 = jnp.float32
BF16 = jnp.bfloat16

EPS = 1e-5
N_META = 16
SSD_HEADDIM = 64
SSD_GROUPS = 4
SSD_STATE = 128
SSD_CONV = 4
CONF_KERNEL = 31
PEER_HEADS = 8
PEER_NKEYS = 128
PEER_TOPK = 16

LANES = 128
SUBLANES = 8
VMEM_LIMIT_BYTES = 56 * 1024 * 1024

SSD_SUB = 128
XBC_HALO = 8
GLU_HALO = 32
DT_PAD = LANES

MIX_TILE = 256
PEER_TOK = 512
PEER_EXP = SUBLANES * PEER_NKEYS
GATE_ROWS = 64


def _dot(a, b):
    return jnp.dot(a, b, preferred_element_type=F32)


def _dot_nt(a, b):
    return lax.dot_general(a, b, (((1,), (1,)), ((), ())), preferred_element_type=F32)


def _dot_tn(a, b):
    return lax.dot_general(a, b, (((0,), (0,)), ((), ())), preferred_element_type=F32)


def _rmsnorm(x, w):
    return x * lax.rsqrt(jnp.mean(x * x, axis=-1, keepdims=True) + EPS) * w


def _silu(x):
    return x * jax.nn.sigmoid(x)


def _gelu(x):
    return 0.5 * x * (1.0 + lax.erf(x * math.sqrt(0.5)))


def _softplus(x):
    return jnp.maximum(x, 0.0) + jnp.log1p(jnp.exp(-jnp.abs(x)))


def _split3(x):
    hi = x.astype(BF16)
    r1 = x - hi.astype(F32)
    mid = r1.astype(BF16)
    lo = (r1 - mid.astype(F32)).astype(BF16)
    return hi, mid, lo


def _expand_heads(a, n_heads):
    q = a.shape[0]
    lane = lax.broadcasted_iota(jnp.int32, (q, LANES), 1)
    blocks = []
    for j in range(n_heads // 2):
        blocks.append(jnp.where(lane < SSD_HEADDIM, a[:, 2 * j:2 * j + 1], a[:, 2 * j + 1:2 * j + 2]))
    return jnp.concatenate(blocks, axis=1)


def _mixer_tile(x, dt_keep, w, state_ref, xbc_ext_ref, glu_ext_ref, *, with_output):
    L, d_model = x.shape
    d_ssd = d_model
    n_heads = d_ssd // SSD_HEADDIM
    d_bc = SSD_GROUPS * SSD_STATE
    d_xbc = d_ssd + 2 * d_bc
    hpg = n_heads // SSD_GROUPS
    gw = hpg * SSD_HEADDIM

    u = _rmsnorm(x, w["norm_mix_w"][...]).astype(BF16)
    o_xbc = d_ssd
    o_conf = o_xbc + d_xbc
    o_dt = o_conf + 2 * d_model
    w_in = w["w_in"]
    xbc_pre = _dot(u, w_in[:, o_xbc:o_conf])
    dt_raw = _dot(u, w_in[:, o_dt:o_dt + DT_PAD])

    xbc_ext_ref[XBC_HALO:XBC_HALO + L, :] = xbc_pre
    acc = jnp.broadcast_to(w["ssd_conv_b"][...], (L, d_xbc))
    for k in range(SSD_CONV):
        off = XBC_HALO - (SSD_CONV - 1) + k
        acc = acc + xbc_ext_ref[pl.ds(off, L), :] * w["ssd_conv_w"][k:k + 1, :]
    xbc_ext_ref[0:XBC_HALO, :] = xbc_ext_ref[L:L + XBC_HALO, :]
    xbc = _silu(acc)

    dt = _softplus(dt_raw + w["ssd_dt_bias"][...])
    if dt_keep is not None:
        dt = dt * dt_keep
    a_neg = -jnp.exp(w["ssd_a_log"][...])
    dta = dt * a_neg

    row = lax.broadcasted_iota(jnp.int32, (SSD_SUB, SSD_SUB), 0)
    col = lax.broadcasted_iota(jnp.int32, (SSD_SUB, SSD_SUB), 1)
    causal = row >= col
    tri = jnp.where(causal, 1.0, 0.0).astype(BF16)
    lane = lax.broadcasted_iota(jnp.int32, (SSD_SUB, LANES), 1)

    y_rows = []
    for j in range(L // SSD_SUB):
        r0 = j * SSD_SUB
        xs = xbc[r0:r0 + SSD_SUB, 0:d_ssd]
        bm = xbc[r0:r0 + SSD_SUB, d_ssd:d_ssd + d_bc].astype(BF16)
        cm = xbc[r0:r0 + SSD_SUB, d_ssd + d_bc:d_xbc].astype(BF16)
        dt_j = dt[r0:r0 + SSD_SUB]
        hi, mid, lo = _split3(dta[r0:r0 + SSD_SUB])
        a_cs = _dot(tri, hi) + _dot(tri, mid) + _dot(tri, lo)
        a_exp = _expand_heads(a_cs, n_heads)
        dt_exp = _expand_heads(dt_j, n_heads)
        a_last = a_exp[SSD_SUB - 1:SSD_SUB, :]
        xdt = xs * dt_exp
        xds = (xdt * jnp.exp(a_last - a_exp)).astype(BF16)
        chunk_decay = jnp.exp(a_last)
        if with_output:
            a_cs_t = a_cs.T
            ea = jnp.exp(a_exp)
            xdt_b = xdt.astype(BF16)
        y_cols = []
        for g in range(SSD_GROUPS):
            bm_g = bm[:, g * SSD_STATE:(g + 1) * SSD_STATE]
            cm_g = cm[:, g * SSD_STATE:(g + 1) * SSD_STATE]
            st_prev = state_ref[g]
            new_st = _dot_tn(bm_g, xds[:, g * gw:(g + 1) * gw])
            state_ref[g] = st_prev * chunk_decay[:, g * gw:(g + 1) * gw] + new_st
            if not with_output:
                continue
            y_off = _dot(cm_g, st_prev.astype(BF16)) * ea[:, g * gw:(g + 1) * gw]
            cb = _dot_nt(cm_g, bm_g)
            for pr in range(hpg // 2):
                ys = []
                for hh in range(2):
                    hd = g * hpg + 2 * pr + hh
                    seg = a_cs[:, hd:hd + 1] - a_cs_t[hd:hd + 1, :]
                    lmat = jnp.where(causal, jnp.exp(jnp.minimum(seg, 0.0)), 0.0)
                    xp = xdt_b[:, (g * hpg + 2 * pr) * SSD_HEADDIM:(g * hpg + 2 * pr + 2) * SSD_HEADDIM]
                    ys.append(_dot((cb * lmat).astype(BF16), xp))
                yd = jnp.where(lane < SSD_HEADDIM, ys[0], ys[1])
                y_cols.append(yd + y_off[:, pr * LANES:(pr + 1) * LANES])
        if with_output:
            y_rows.append(jnp.concatenate(y_cols, axis=1) + w["ssd_d"][...] * xs)

    conf = _dot(u, w_in[:, o_conf:o_dt])
    glu = conf[:, :d_model] * jax.nn.sigmoid(conf[:, d_model:])
    glu_ext_ref[GLU_HALO:GLU_HALO + L, :] = glu
    if not with_output:
        glu_ext_ref[0:GLU_HALO, :] = glu_ext_ref[L:L + GLU_HALO, :]
        return None

    z = _dot(u, w_in[:, 0:d_ssd])
    y = jnp.concatenate(y_rows, axis=0) if len(y_rows) > 1 else y_rows[0]
    y = y * _silu(z)
    parts = []
    for g in range(SSD_GROUPS):
        yg = y[:, g * gw:(g + 1) * gw]
        parts.append(yg * lax.rsqrt(jnp.mean(yg * yg, axis=-1, keepdims=True) + EPS))
    y_ssd = jnp.concatenate(parts, axis=1) * w["ssd_norm_w"][...]

    hc = jnp.broadcast_to(w["conf_conv_b"][...], (L, d_model))
    for k in range(CONF_KERNEL):
        off = GLU_HALO - (CONF_KERNEL - 1) + k
        hc = hc + glu_ext_ref[pl.ds(off, L), :] * w["conf_conv_w"][k:k + 1, :]
    glu_ext_ref[0:GLU_HALO, :] = glu_ext_ref[L:L + GLU_HALO, :]
    mu = jnp.mean(hc, axis=-1, keepdims=True)
    hcc = hc - mu
    var = jnp.mean(hcc * hcc, axis=-1, keepdims=True)
    hn = hcc * lax.rsqrt(var + EPS) * w["conf_ln_g"][...] + w["conf_ln_b"][...]
    y_conf = _silu(hn)

    mix = jnp.concatenate([y_ssd, y_conf], axis=1).astype(BF16)
    return x + _dot(mix, w["w_out"][...])


_MIX_WEIGHTS = ("norm_mix_w", "w_in", "ssd_conv_w", "ssd_conv_b", "ssd_dt_bias", "ssd_a_log",
                "ssd_d", "ssd_norm_w", "conf_conv_w", "conf_conv_b", "conf_ln_g", "conf_ln_b", "w_out")


def _meta_kernel(x_ref, keep_ref, *refs):
    nw = len(_MIX_WEIGHTS)
    w = dict(zip(_MIX_WEIGHTS, refs[:nw]))
    state_ref, xbc_halo_ref, glu_halo_ref, xbc_ext_ref, glu_ext_ref = refs[nw:]
    state_ref[...] = jnp.zeros(state_ref.shape, F32)
    xbc_ext_ref[0:XBC_HALO, :] = jnp.zeros((XBC_HALO, xbc_ext_ref.shape[1]), F32)
    glu_ext_ref[0:GLU_HALO, :] = jnp.zeros((GLU_HALO, glu_ext_ref.shape[1]), F32)
    _mixer_tile(x_ref[...], keep_ref[...], w, state_ref, xbc_ext_ref, glu_ext_ref, with_output=False)
    xbc_halo_ref[...] = xbc_ext_ref[0:XBC_HALO, :]
    glu_halo_ref[...] = glu_ext_ref[0:GLU_HALO, :]


def _mixer_kernel(x_ref, state0_ref, xbc_halo0_ref, glu_halo0_ref, *refs):
    nw = len(_MIX_WEIGHTS)
    w = dict(zip(_MIX_WEIGHTS, refs[:nw]))
    out_ref, state_ref, xbc_ext_ref, glu_ext_ref = refs[nw:]

    @pl.when(pl.program_id(1) == 0)
    def _():
        state_ref[...] = state0_ref[...]
        xbc_ext_ref[0:XBC_HALO, :] = xbc_halo0_ref[...]
        glu_ext_ref[0:GLU_HALO, :] = glu_halo0_ref[...]

    out_ref[...] = _mixer_tile(x_ref[...], None, w, state_ref, xbc_ext_ref, glu_ext_ref,
                               with_output=True)


def _const_spec(shape):
    nd = len(shape)
    return pl.BlockSpec(shape, lambda *_: (0,) * nd, pipeline_mode=pl.Buffered(1))


def _mixer_calls(x, meta_tile, keep, wts):
    b, seq, d = x.shape
    n_heads = d // SSD_HEADDIM
    gw = (n_heads // SSD_GROUPS) * SSD_HEADDIM
    d_xbc = d + 2 * SSD_GROUPS * SSD_STATE
    w_list = [wts[k] for k in _MIX_WEIGHTS]
    state_shape = (SSD_GROUPS, SSD_STATE, gw)

    lm = meta_tile.shape[0]
    state0, xbc_halo0, glu_halo0 = pl.pallas_call(
        _meta_kernel,
        out_shape=(jax.ShapeDtypeStruct(state_shape, F32),
                   jax.ShapeDtypeStruct((XBC_HALO, d_xbc), F32),
                   jax.ShapeDtypeStruct((GLU_HALO, d), F32)),
        scratch_shapes=[pltpu.VMEM((XBC_HALO + lm, d_xbc), F32),
                        pltpu.VMEM((GLU_HALO + lm, d), F32)],
        compiler_params=pltpu.CompilerParams(vmem_limit_bytes=VMEM_LIMIT_BYTES),
        name="meta_prologue",
    )(meta_tile, keep, *w_list)

    lt = MIX_TILE
    assert seq % lt == 0
    return pl.pallas_call(
        _mixer_kernel,
        grid=(b, seq // lt),
        in_specs=[pl.BlockSpec((None, lt, d), lambda i, c: (i, c, 0)),
                  _const_spec(state_shape), _const_spec((XBC_HALO, d_xbc)), _const_spec((GLU_HALO, d))]
                 + [_const_spec(a.shape) for a in w_list],
        out_specs=pl.BlockSpec((None, lt, d), lambda i, c: (i, c, 0)),
        out_shape=jax.ShapeDtypeStruct((b, seq, d), F32),
        scratch_shapes=[pltpu.VMEM(state_shape, F32),
                        pltpu.VMEM((XBC_HALO + lt, d_xbc), F32),
                        pltpu.VMEM((GLU_HALO + lt, d), F32)],
        compiler_params=pltpu.CompilerParams(
            dimension_semantics=("arbitrary", "arbitrary"), vmem_limit_bytes=VMEM_LIMIT_BYTES),
        name="mixer",
    )(x, state0, xbc_halo0, glu_halo0, *w_list)


def _sort_desc(v):
    v = list(v)
    n = len(v)
    k = 2
    while k <= n:
        j = k // 2
        while j >= 1:
            for i in range(n):
                l = i ^ j
                if l > i:
                    hi, lo = jnp.maximum(v[i], v[l]), jnp.minimum(v[i], v[l])
                    v[i], v[l] = (hi, lo) if (i & k) == 0 else (lo, hi)
            j //= 2
        k *= 2
    return v


def _merge_top(a, b, *, sort=True):
    n = len(a)
    v = [jnp.maximum(a[i], b[n - 1 - i]) for i in range(n)]
    if not sort:
        return v
    j = n // 2
    while j >= 1:
        for i in range(n):
            l = i ^ j
            if l > i:
                v[i], v[l] = jnp.maximum(v[i], v[l]), jnp.minimum(v[i], v[l])
        j //= 2
    return v


def _top_sorted(s):
    n = s.shape[0] // SUBLANES
    v = _sort_desc([s[SUBLANES * i:SUBLANES * (i + 1), :] for i in range(n)])
    for shift in (4, 2, 1):
        v = _merge_top(v, [pltpu.roll(a, shift, axis=0) for a in v])
    return v


def _selection_stats(v1, v2):
    t = v1[0].shape[1]
    sub = lax.broadcasted_iota(jnp.int32, (SUBLANES, t), 0)
    lists = []
    for q in range(PEER_TOPK // SUBLANES):
        col = v1[SUBLANES * q]
        for s in range(1, SUBLANES):
            col = jnp.where(sub == s, v1[SUBLANES * q + s], col)
        lists.append([col + b for b in v2])
    m = _merge_top(lists[0], lists[1])
    m = _merge_top(m, [pltpu.roll(a, 4, axis=0) for a in m])
    m = _merge_top(m, [pltpu.roll(a, 2, axis=0) for a in m])
    m = _merge_top(m, [pltpu.roll(a, 1, axis=0) for a in m], sort=False)
    top = v1[0] + v2[0]
    tau = functools.reduce(jnp.minimum, m)
    z = functools.reduce(lambda a, b: a + b, [jnp.exp(c - top) for c in m])
    return tau, z


def _peer_kernel(h_ref, nfw_ref, wq_ref, k1_ref, k2_ref, wd_ref, wu_ref, nlw_ref, out_ref,
                 xt_ref, acc_ref, a_ref, s1_ref, e1_ref, s2_ref, e2_ref, tau_ref):
    e = pl.program_id(1)
    tt = h_ref.shape[0]
    ec = wd_ref.shape[0]
    dk = k1_ref.shape[2]

    @pl.when(e == 0)
    def _():
        xn = _rmsnorm(h_ref[...], nfw_ref[...])
        xt_ref[...] = xn.T.astype(BF16)
        qt = _dot(wq_ref[...], xt_ref[...]).astype(BF16)
        for h in range(PEER_HEADS):
            s1 = _dot(k1_ref[h], qt[2 * dk * h:2 * dk * h + dk, :])
            s2 = _dot(k2_ref[h], qt[2 * dk * h + dk:2 * dk * (h + 1), :])
            for lt in range(tt // LANES):
                ls = slice(lt * LANES, (lt + 1) * LANES)
                v1 = _top_sorted(s1[:, ls])
                v2 = _top_sorted(s2[:, ls])
                tau, z = _selection_stats(v1, v2)
                tau_ref[h, :, ls] = tau
                e1 = jnp.exp(s1[:, ls] - v1[0][0:1, :])
                for blk in range(PEER_NKEYS // SUBLANES):
                    rows = slice(blk * SUBLANES, (blk + 1) * SUBLANES)
                    s1_ref[h, blk, :, ls] = s1[rows, ls]
                    e1_ref[h, blk, :, ls] = e1[rows, :]
                s2_ref[h, lt] = s2[:, ls]
                e2_ref[h, lt] = jnp.exp(s2[:, ls] - v2[0][0:1, :]) / z[0:1, :]
        acc_ref[...] = jnp.zeros(acc_ref.shape, F32)

    s = _dot(wd_ref[...], xt_ref[...])
    assert ec == SUBLANES * PEER_NKEYS
    bshape = (GATE_ROWS, LANES)
    for lt in range(tt // LANES):
        ls = pl.ds(lt * LANES, LANES)
        tau = [jnp.broadcast_to(tau_ref[h, 0:1, ls], bshape) for h in range(PEER_HEADS)]
        for k in range(SUBLANES):
            s1b = [jnp.broadcast_to(s1_ref[h, e, k:k + 1, ls], bshape) for h in range(PEER_HEADS)]
            e1b = [jnp.broadcast_to(e1_ref[h, e, k:k + 1, ls], bshape) for h in range(PEER_HEADS)]
            for blk in range(PEER_NKEYS // GATE_ROWS):
                rs = pl.ds(blk * GATE_ROWS, GATE_ROWS)
                gate = None
                for h in range(PEER_HEADS):
                    term = jnp.where(s1b[h] + s2_ref[h, lt, rs, :] >= tau[h], e2_ref[h, lt, rs, :], 0.0) * e1b[h]
                    gate = term if gate is None else gate + term
                r0 = k * PEER_NKEYS + blk * GATE_ROWS
                pre = s[r0:r0 + GATE_ROWS, lt * LANES:(lt + 1) * LANES]
                a_ref[r0:r0 + GATE_ROWS, ls] = (_gelu(pre) * gate).astype(BF16)
    acc_ref[...] += _dot(wu_ref[...], a_ref[...])

    @pl.when(e == pl.num_programs(1) - 1)
    def _():
        out_ref[...] = _rmsnorm(h_ref[...] + acc_ref[...].T, nlw_ref[...])


def _peer_call(h, nfw, wq_t, k1, k2, wd, wu_t, nlw):
    t, d = h.shape
    n_exp = wd.shape[0]
    tt, ec = PEER_TOK, PEER_EXP
    assert t % tt == 0 and n_exp % ec == 0
    gate_shape = (PEER_HEADS, tt // LANES, PEER_NKEYS, LANES)
    row_shape = (PEER_HEADS, PEER_NKEYS // SUBLANES, SUBLANES, tt)
    return pl.pallas_call(
        _peer_kernel,
        grid=(t // tt, n_exp // ec),
        in_specs=[pl.BlockSpec((tt, d), lambda i, e: (i, 0)),
                  _const_spec(nfw.shape), _const_spec(wq_t.shape), _const_spec(k1.shape),
                  _const_spec(k2.shape),
                  pl.BlockSpec((ec, d), lambda i, e: (e, 0)),
                  pl.BlockSpec((d, ec), lambda i, e: (0, e)),
                  _const_spec(nlw.shape)],
        out_specs=pl.BlockSpec((tt, d), lambda i, e: (i, 0)),
        out_shape=jax.ShapeDtypeStruct((t, d), F32),
        scratch_shapes=[pltpu.VMEM((d, tt), BF16),
                        pltpu.VMEM((d, tt), F32),
                        pltpu.VMEM((ec, tt), BF16),
                        pltpu.VMEM(row_shape, F32), pltpu.VMEM(row_shape, F32),
                        pltpu.VMEM(gate_shape, F32), pltpu.VMEM(gate_shape, F32),
                        pltpu.VMEM((PEER_HEADS, SUBLANES, tt), F32)],
        compiler_params=pltpu.CompilerParams(
            dimension_semantics=("arbitrary", "arbitrary"), vmem_limit_bytes=VMEM_LIMIT_BYTES),
        name="peer",
    )(h, nfw, wq_t, k1, k2, wd, wu_t, nlw)


def _pad_lanes(a, width):
    return jnp.pad(a, [(0, 0)] * (a.ndim - 1) + [(0, width - a.shape[-1])])


def kernel(x, meta_tokens, norm_mix_w, w_in, ssd_conv_w, ssd_conv_b, ssd_dt_bias, ssd_A_log, ssd_D,
           ssd_norm_w, conf_conv_w, conf_conv_b, conf_ln_g, conf_ln_b, w_out, norm_ffn_w,
           peer_w_query, peer_sub_keys_1, peer_sub_keys_2, peer_w_down, peer_w_up, norm_final_w):
    b, seq, d = x.shape
    assert norm_mix_w.shape[0] == 1, "one layer"
    d_xbc = d + 2 * SSD_GROUPS * SSD_STATE
    n_heads = d // SSD_HEADDIM
    row = lambda a: a.reshape(1, -1).astype(F32)

    wi = w_in[0]
    o1, o2, o3 = d, d + d_xbc, d + d_xbc + n_heads
    w_in_r = jnp.concatenate([wi[:, :o1], wi[:, o1:o2], wi[:, o3:], _pad_lanes(wi[:, o2:o3], DT_PAD)],
                             axis=1).astype(BF16)
    wts = {
        "norm_mix_w": row(norm_mix_w[0]), "w_in": w_in_r,
        "ssd_conv_w": ssd_conv_w[0].astype(F32), "ssd_conv_b": row(ssd_conv_b[0]),
        "ssd_dt_bias": _pad_lanes(row(ssd_dt_bias[0]), DT_PAD),
        "ssd_a_log": _pad_lanes(row(ssd_A_log[0]), DT_PAD),
        "ssd_d": row(jnp.repeat(ssd_D[0], SSD_HEADDIM)), "ssd_norm_w": row(ssd_norm_w[0]),
        "conf_conv_w": conf_conv_w[0].astype(F32), "conf_conv_b": row(conf_conv_b[0]),
        "conf_ln_g": row(conf_ln_g[0]), "conf_ln_b": row(conf_ln_b[0]),
        "w_out": w_out[0].astype(BF16),
    }

    pad = SSD_SUB - N_META
    meta_tile = jnp.pad(meta_tokens.astype(F32), [(pad, 0), (0, 0)])
    keep = jnp.pad(jnp.ones((N_META, DT_PAD), F32), [(pad, 0), (0, 0)])
    h1 = _mixer_calls(x.astype(F32), meta_tile, keep, wts)

    out = _peer_call(
        h1.reshape(b * seq, d), row(norm_ffn_w[0]),
        peer_w_query[0].T.astype(BF16), peer_sub_keys_1[0].astype(BF16), peer_sub_keys_2[0].astype(BF16),
        peer_w_down[0].astype(BF16), peer_w_up[0].T.astype(BF16), row(norm_final_w))
    return out.reshape(b, seq, d).astype(x.dtype)
```

```python
import functools
import math

import jax
import jax.numpy as jnp
from jax import lax
from jax.experimental import pallas as pl
from jax.experimental.pallas import tpu as pltpu

F32 = jnp.float32
BF16 = jnp.bfloat16

EPS = 1e-5
N_META = 16
SSD_HEADDIM = 64
SSD_GROUPS = 4
SSD_STATE = 128
SSD_CONV = 4
CONF_KERNEL = 31
PEER_HEADS = 8
PEER_NKEYS = 128
PEER_TOPK = 16
LOG2E = math.log2(math.e)

LANES = 128
SUBLANES = 8
VMEM_LIMIT_BYTES = 56 * 1024 * 1024

SSD_SUB = 128
XBC_HALO = 8
GLU_HALO = 32
DT_PAD = LANES

MIX_TILE = 256
CONV_ROWS = 64
PEER_TOK = 512
PEER_EXP = SUBLANES * PEER_NKEYS
GATE_ROWS = 64


def _dot(a, b):
    return jnp.dot(a, b, preferred_element_type=F32)


def _dot_nt(a, b):
    return lax.dot_general(a, b, (((1,), (1,)), ((), ())), preferred_element_type=F32)


def _dot_tn(a, b):
    return lax.dot_general(a, b, (((0,), (0,)), ((), ())), preferred_element_type=F32)


def _rmsnorm(x, w):
    return x * lax.rsqrt(jnp.mean(x * x, axis=-1, keepdims=True) + EPS) * w


def _silu(x):
    return x * jax.nn.sigmoid(x)


def _softplus(x):
    return jnp.maximum(x, 0.0) + jnp.log1p(jnp.exp(-jnp.abs(x)))


def _split3(x):
    hi = x.astype(BF16)
    r1 = x - hi.astype(F32)
    mid = r1.astype(BF16)
    lo = (r1 - mid.astype(F32)).astype(BF16)
    return hi, mid, lo


def _expand_heads(a, n_heads):
    q = a.shape[0]
    lane = lax.broadcasted_iota(jnp.int32, (q, LANES), 1)
    blocks = []
    for j in range(n_heads // 2):
        blocks.append(jnp.where(lane < SSD_HEADDIM, a[:, 2 * j:2 * j + 1], a[:, 2 * j + 1:2 * j + 2]))
    return jnp.concatenate(blocks, axis=1)


def _mixer_tile(x, dt_keep, w, state_ref, xbc_ext_ref, glu_ext_ref, conv_scratch=None):
    with_output = conv_scratch is not None
    L, d_model = x.shape
    d_ssd = d_model
    n_heads = d_ssd // SSD_HEADDIM
    d_bc = SSD_GROUPS * SSD_STATE
    d_xbc = d_ssd + 2 * d_bc
    hpg = n_heads // SSD_GROUPS
    gw = hpg * SSD_HEADDIM

    u = _rmsnorm(x, w["norm_mix_w"][...]).astype(BF16)
    o_xbc = d_ssd
    o_conf = o_xbc + d_xbc
    o_dt = o_conf + 2 * d_model
    w_in = w["w_in"]
    xbc_pre = _dot(u, w_in[:, o_xbc:o_conf])
    dt_raw = _dot(u, w_in[:, o_dt:o_dt + DT_PAD])

    xbc_ext_ref[XBC_HALO:XBC_HALO + L, :] = xbc_pre
    acc = jnp.broadcast_to(w["ssd_conv_b"][...], (L, d_xbc))
    for k in range(SSD_CONV):
        off = XBC_HALO - (SSD_CONV - 1) + k
        acc = acc + xbc_ext_ref[pl.ds(off, L), :] * w["ssd_conv_w"][k:k + 1, :]
    xbc_ext_ref[0:XBC_HALO, :] = xbc_ext_ref[L:L + XBC_HALO, :]
    xbc = _silu(acc)

    dt = _softplus(dt_raw + w["ssd_dt_bias"][...])
    if dt_keep is not None:
        dt = dt * dt_keep
    a_neg = -jnp.exp(w["ssd_a_log"][...])
    dta = dt * a_neg

    row = lax.broadcasted_iota(jnp.int32, (SSD_SUB, SSD_SUB), 0)
    col = lax.broadcasted_iota(jnp.int32, (SSD_SUB, SSD_SUB), 1)
    causal = row >= col
    tri = jnp.where(causal, 1.0, 0.0).astype(BF16)
    lane = lax.broadcasted_iota(jnp.int32, (SSD_SUB, LANES), 1)

    y_rows = []
    for j in range(L // SSD_SUB):
        r0 = j * SSD_SUB
        xs = xbc[r0:r0 + SSD_SUB, 0:d_ssd]
        bm = xbc[r0:r0 + SSD_SUB, d_ssd:d_ssd + d_bc].astype(BF16)
        cm = xbc[r0:r0 + SSD_SUB, d_ssd + d_bc:d_xbc].astype(BF16)
        dt_j = dt[r0:r0 + SSD_SUB]
        hi, mid, lo = _split3(dta[r0:r0 + SSD_SUB])
        a_cs = _dot(tri, hi) + _dot(tri, mid) + _dot(tri, lo)
        a_exp = _expand_heads(a_cs, n_heads)
        dt_exp = _expand_heads(dt_j, n_heads)
        a_last = a_exp[SSD_SUB - 1:SSD_SUB, :]
        xdt = xs * dt_exp
        xds = (xdt * jnp.exp(a_last - a_exp)).astype(BF16)
        chunk_decay = jnp.exp(a_last)
        if with_output:
            a_cs_t = a_cs.T
            ea = jnp.exp(a_exp)
            xdt_b = xdt.astype(BF16)
        y_cols = []
        for g in range(SSD_GROUPS):
            bm_g = bm[:, g * SSD_STATE:(g + 1) * SSD_STATE]
            cm_g = cm[:, g * SSD_STATE:(g + 1) * SSD_STATE]
            st_prev = state_ref[g]
            new_st = _dot_tn(bm_g, xds[:, g * gw:(g + 1) * gw])
            state_ref[g] = st_prev * chunk_decay[:, g * gw:(g + 1) * gw] + new_st
            if not with_output:
                continue
            y_off = _dot(cm_g, st_prev.astype(BF16)) * ea[:, g * gw:(g + 1) * gw]
            cb = _dot_nt(cm_g, bm_g)
            for pr in range(hpg // 2):
                ys = []
                for hh in range(2):
                    hd = g * hpg + 2 * pr + hh
                    seg = a_cs[:, hd:hd + 1] - a_cs_t[hd:hd + 1, :]
                    lmat = jnp.where(causal, jnp.exp(jnp.minimum(seg, 0.0)), 0.0)
                    xp = xdt_b[:, (g * hpg + 2 * pr) * SSD_HEADDIM:(g * hpg + 2 * pr + 2) * SSD_HEADDIM]
                    ys.append(_dot((cb * lmat).astype(BF16), xp))
                yd = jnp.where(lane < SSD_HEADDIM, ys[0], ys[1])
                y_cols.append(yd + y_off[:, pr * LANES:(pr + 1) * LANES])
        if with_output:
            y_rows.append(jnp.concatenate(y_cols, axis=1) + w["ssd_d"][...] * xs)

    conf = _dot(u, w_in[:, o_conf:o_dt])
    glu = conf[:, :d_model] * jax.nn.sigmoid(conf[:, d_model:])
    for lt in range(d_model // LANES):
        glu_ext_ref[lt, GLU_HALO:GLU_HALO + L, :] = glu[:, lt * LANES:(lt + 1) * LANES]
    if not with_output:
        glu_ext_ref[:, 0:GLU_HALO, :] = glu_ext_ref[:, L:L + GLU_HALO, :]
        return None

    z = _dot(u, w_in[:, 0:d_ssd])
    y = jnp.concatenate(y_rows, axis=0) if len(y_rows) > 1 else y_rows[0]
    y = y * _silu(z)
    parts = []
    for g in range(SSD_GROUPS):
        yg = y[:, g * gw:(g + 1) * gw]
        parts.append(yg * lax.rsqrt(jnp.mean(yg * yg, axis=-1, keepdims=True) + EPS))
    y_ssd = jnp.concatenate(parts, axis=1) * w["ssd_norm_w"][...]

    shift_ref, hc_ref = conv_scratch
    n_sh = shift_ref.shape[2]
    for r in range(1, SUBLANES):
        shift_ref[r - 1] = glu_ext_ref[:, pl.ds(r, n_sh), :]
    for lt in range(d_model // LANES):
        ls = pl.ds(lt * LANES, LANES)
        for rb in range(L // CONV_ROWS):
            acc = jnp.broadcast_to(w["conf_conv_b"][:, ls], (CONV_ROWS, LANES))
            for k in range(CONF_KERNEL):
                off = GLU_HALO - (CONF_KERNEL - 1) + k
                r = off % SUBLANES
                src = glu_ext_ref.at[lt] if r == 0 else shift_ref.at[r - 1, lt]
                acc = acc + src[pl.ds(off - r + rb * CONV_ROWS, CONV_ROWS), :] * w["conf_conv_w"][k:k + 1, ls]
            hc_ref[pl.ds(rb * CONV_ROWS, CONV_ROWS), ls] = acc
    glu_ext_ref[:, 0:GLU_HALO, :] = glu_ext_ref[:, L:L + GLU_HALO, :]
    hc = hc_ref[...]
    mu = jnp.mean(hc, axis=-1, keepdims=True)
    hcc = hc - mu
    var = jnp.mean(hcc * hcc, axis=-1, keepdims=True)
    hn = hcc * lax.rsqrt(var + EPS) * w["conf_ln_g"][...] + w["conf_ln_b"][...]
    y_conf = _silu(hn)

    mix = jnp.concatenate([y_ssd, y_conf], axis=1).astype(BF16)
    return x + _dot(mix, w["w_out"][...])


_MIX_WEIGHTS = ("norm_mix_w", "w_in", "ssd_conv_w", "ssd_conv_b", "ssd_dt_bias", "ssd_a_log",
                "ssd_d", "ssd_norm_w", "conf_conv_w", "conf_conv_b", "conf_ln_g", "conf_ln_b", "w_out")


def _meta_kernel(x_ref, keep_ref, *refs):
    nw = len(_MIX_WEIGHTS)
    w = dict(zip(_MIX_WEIGHTS, refs[:nw]))
    state_ref, xbc_halo_ref, glu_halo_ref, xbc_ext_ref, glu_ext_ref = refs[nw:]
    state_ref[...] = jnp.zeros(state_ref.shape, F32)
    xbc_ext_ref[0:XBC_HALO, :] = jnp.zeros((XBC_HALO, xbc_ext_ref.shape[1]), F32)
    glu_ext_ref[:, 0:GLU_HALO, :] = jnp.zeros(glu_halo_ref.shape, F32)
    _mixer_tile(x_ref[...], keep_ref[...], w, state_ref, xbc_ext_ref, glu_ext_ref)
    xbc_halo_ref[...] = xbc_ext_ref[0:XBC_HALO, :]
    glu_halo_ref[...] = glu_ext_ref[:, 0:GLU_HALO, :]


def _mixer_kernel(x_ref, state0_ref, xbc_halo0_ref, glu_halo0_ref, *refs):
    nw = len(_MIX_WEIGHTS)
    w = dict(zip(_MIX_WEIGHTS, refs[:nw]))
    out_ref, state_ref, xbc_ext_ref, glu_ext_ref, shift_ref, hc_ref = refs[nw:]

    @pl.when(pl.program_id(1) == 0)
    def _():
        state_ref[...] = state0_ref[...]
        xbc_ext_ref[0:XBC_HALO, :] = xbc_halo0_ref[...]
        glu_ext_ref[:, 0:GLU_HALO, :] = glu_halo0_ref[...]

    out_ref[...] = _mixer_tile(x_ref[...], None, w, state_ref, xbc_ext_ref, glu_ext_ref,
                               (shift_ref, hc_ref))


def _const_spec(shape):
    nd = len(shape)
    return pl.BlockSpec(shape, lambda *_: (0,) * nd, pipeline_mode=pl.Buffered(1))


def _mixer_calls(x, meta_tile, keep, wts):
    b, seq, d = x.shape
    n_heads = d // SSD_HEADDIM
    gw = (n_heads // SSD_GROUPS) * SSD_HEADDIM
    d_xbc = d + 2 * SSD_GROUPS * SSD_STATE
    w_list = [wts[k] for k in _MIX_WEIGHTS]
    state_shape = (SSD_GROUPS, SSD_STATE, gw)
    glu_halo_shape = (d // LANES, GLU_HALO, LANES)

    lm = meta_tile.shape[0]
    state0, xbc_halo0, glu_halo0 = pl.pallas_call(
        _meta_kernel,
        out_shape=(jax.ShapeDtypeStruct(state_shape, F32),
                   jax.ShapeDtypeStruct((XBC_HALO, d_xbc), F32),
                   jax.ShapeDtypeStruct(glu_halo_shape, F32)),
        scratch_shapes=[pltpu.VMEM((XBC_HALO + lm, d_xbc), F32),
                        pltpu.VMEM((d // LANES, GLU_HALO + lm, LANES), F32)],
        compiler_params=pltpu.CompilerParams(vmem_limit_bytes=VMEM_LIMIT_BYTES),
        name="meta_prologue",
    )(meta_tile, keep, *w_list)

    lt = MIX_TILE
    assert seq % lt == 0
    return pl.pallas_call(
        _mixer_kernel,
        grid=(b, seq // lt),
        in_specs=[pl.BlockSpec((None, lt, d), lambda i, c: (i, c, 0)),
                  _const_spec(state_shape), _const_spec((XBC_HALO, d_xbc)), _const_spec(glu_halo_shape)]
                 + [_const_spec(a.shape) for a in w_list],
        out_specs=pl.BlockSpec((None, lt, d), lambda i, c: (i, c, 0)),
        out_shape=jax.ShapeDtypeStruct((b, seq, d), F32),
        scratch_shapes=[pltpu.VMEM(state_shape, F32),
                        pltpu.VMEM((XBC_HALO + lt, d_xbc), F32),
                        pltpu.VMEM((d // LANES, GLU_HALO + lt, LANES), F32),
                        pltpu.VMEM((SUBLANES - 1, d // LANES, GLU_HALO + lt - SUBLANES, LANES), F32),
                        pltpu.VMEM((lt, d), F32)],
        compiler_params=pltpu.CompilerParams(
            dimension_semantics=("arbitrary", "arbitrary"), vmem_limit_bytes=VMEM_LIMIT_BYTES),
        name="mixer",
    )(x, state0, xbc_halo0, glu_halo0, *w_list)


def _sort_desc(v):
    v = list(v)
    n = len(v)
    k = 2
    while k <= n:
        j = k // 2
        while j >= 1:
            for i in range(n):
                l = i ^ j
                if l > i:
                    hi, lo = jnp.maximum(v[i], v[l]), jnp.minimum(v[i], v[l])
                    v[i], v[l] = (hi, lo) if (i & k) == 0 else (lo, hi)
            j //= 2
        k *= 2
    return v


def _merge_top(a, b, *, sort=True):
    n = len(a)
    v = [jnp.maximum(a[i], b[n - 1 - i]) for i in range(n)]
    if not sort:
        return v
    j = n // 2
    while j >= 1:
        for i in range(n):
            l = i ^ j
            if l > i:
                v[i], v[l] = jnp.maximum(v[i], v[l]), jnp.minimum(v[i], v[l])
        j //= 2
    return v


def _top_sorted(s):
    n = s.shape[0] // SUBLANES
    v = _sort_desc([s[SUBLANES * i:SUBLANES * (i + 1), :] for i in range(n)])
    for shift in (4, 2, 1):
        v = _merge_top(v, [pltpu.roll(a, shift, axis=0) for a in v])
    return v


def _top_pair_sums(v1, v2):
    t = v1[0].shape[1]
    sub = lax.broadcasted_iota(jnp.int32, (SUBLANES, t), 0)
    lists = []
    for q in range(PEER_TOPK // SUBLANES):
        col = v1[SUBLANES * q]
        for s in range(1, SUBLANES):
            col = jnp.where(sub == s, v1[SUBLANES * q + s], col)
        lists.append([col + b for b in v2])
    m = _merge_top(lists[0], lists[1])
    m = _merge_top(m, [pltpu.roll(a, 4, axis=0) for a in m])
    m = _merge_top(m, [pltpu.roll(a, 2, axis=0) for a in m])
    return _merge_top(m, [pltpu.roll(a, 1, axis=0) for a in m], sort=False)


def _peer_scores(xt_ref, wq_ref, k1_ref, k2_ref, s1_ref, s2_ref, tau_ref):
    tt = xt_ref.shape[1]
    dk = k1_ref.shape[2]
    qt = _dot(wq_ref[...], xt_ref[...]).astype(BF16)
    for h in range(PEER_HEADS):
        s1 = _dot(k1_ref[h], qt[2 * dk * h:2 * dk * h + dk, :]) * LOG2E
        s2 = _dot(k2_ref[h], qt[2 * dk * h + dk:2 * dk * (h + 1), :]) * LOG2E
        for lt in range(tt // LANES):
            ls = slice(lt * LANES, (lt + 1) * LANES)
            v1 = _top_sorted(s1[:, ls])
            v2 = _top_sorted(s2[:, ls])
            top = v1[0] + v2[0]
            z = functools.reduce(lambda a, b: a + b, [jnp.exp2(x - top) for x in _top_pair_sums(v1, v2)])
            c = jnp.broadcast_to((top + jnp.log2(z) + 1.0)[0:1, :], top.shape)
            tau_ref[h, :, ls] = functools.reduce(jnp.minimum, _top_pair_sums([v - c for v in v1], v2))
            s1y = s1[:, ls] - c[0:1, :]
            for blk in range(PEER_NKEYS // SUBLANES):
                s1_ref[h, blk, :, ls] = s1y[blk * SUBLANES:(blk + 1) * SUBLANES, :]
            s2_ref[h, lt] = s2[:, ls]


def _peer_activations(lt, chunk, s_ref, a_ref, s1_ref, s2_ref, tau_ref):
    ls = pl.ds(lt * LANES, LANES)
    bshape = (GATE_ROWS, LANES)
    tau = [jnp.broadcast_to(tau_ref[h, 0:1, ls], bshape) for h in range(PEER_HEADS)]
    for k in range(SUBLANES):
        s1b = [jnp.broadcast_to(s1_ref[h, chunk, k:k + 1, ls], bshape) for h in range(PEER_HEADS)]
        for blk in range(PEER_NKEYS // GATE_ROWS):
            rs = pl.ds(blk * GATE_ROWS, GATE_ROWS)
            gate = None
            for h in range(PEER_HEADS):
                y = s1b[h] + s2_ref[h, lt, rs, :]
                term = jnp.where(y >= tau[h], jnp.exp2(y), 0.0)
                gate = term if gate is None else gate + term
            rows = pl.ds(k * PEER_NKEYS + blk * GATE_ROWS, GATE_ROWS)
            pre = s_ref[rows, ls]
            a_ref[rows, ls] = (pre * (1.0 + lax.erf(pre * math.sqrt(0.5))) * gate).astype(BF16)


def _peer_kernel(hcur_ref, hres_ref, nfw_ref, wq_ref, k1_ref, k2_ref, wd_ref, wu_ref, nlw_ref, out_ref,
                 xt_ref, acc_ref, sa_ref, sb_ref, aa_ref, ab_ref, s1_ref, s2_ref, tau_ref,
                 *, pairs_per_tile, n_pairs):
    g = pl.program_id(0)
    tt, d = hcur_ref.shape
    ec = wd_ref.shape[0] // 2
    assert ec == SUBLANES * PEER_NKEYS
    tile_start = g % pairs_per_tile == 0
    in_range = g < n_pairs

    @pl.when(g == 0)
    def _():
        for ref in (sb_ref, aa_ref, acc_ref, s1_ref, s2_ref, tau_ref):
            ref[...] = jnp.zeros(ref.shape, ref.dtype)

    @pl.when(jnp.logical_and(tile_start, in_range))
    def _():
        xn = _rmsnorm(hcur_ref[...], nfw_ref[...])
        xt_ref[...] = xn.T.astype(BF16)

    def step(sub, s_new, s_old, a_new, a_old, chunk_old):
        wd = wd_ref.at[pl.ds(sub * ec, ec), :]
        wu = wu_ref.at[:, pl.ds(sub * ec, ec)]
        s_new[...] = _dot(wd[...], xt_ref[...])
        acc_ref[...] += _dot(wu[...], a_old[...])
        for lt in range(tt // LANES):
            _peer_activations(lt, chunk_old, s_old, a_new, s1_ref, s2_ref, tau_ref)

    chunks = 2 * pairs_per_tile
    step(0, sa_ref, sb_ref, ab_ref, aa_ref, jnp.maximum(2 * g - 1, 0) % chunks)

    @pl.when(jnp.logical_and(tile_start, in_range))
    def _():
        _peer_scores(xt_ref, wq_ref, k1_ref, k2_ref, s1_ref, s2_ref, tau_ref)

    step(1, sb_ref, sa_ref, aa_ref, ab_ref, (2 * g) % chunks)

    @pl.when(jnp.logical_and(tile_start, g > 0))
    def _():
        out_ref[...] = _rmsnorm(hres_ref[...] + acc_ref[...].T, nlw_ref[...])
        acc_ref[...] = jnp.zeros(acc_ref.shape, F32)


def _peer_call(h, nfw, wq_t, k1, k2, wd, wu_t, nlw):
    t, d = h.shape
    n_exp = wd.shape[0]
    tt, ec = PEER_TOK, PEER_EXP
    assert t % tt == 0 and n_exp % (2 * ec) == 0
    pairs_per_tile = n_exp // (2 * ec)
    n_pairs = (t // tt) * pairs_per_tile
    cur_pair = lambda g: jnp.minimum(g, n_pairs - 1)
    done_tile = lambda g: jnp.maximum(2 * g - 1, 0) // (2 * pairs_per_tile)
    gate_shape = (PEER_HEADS, tt // LANES, PEER_NKEYS, LANES)
    row_shape = (PEER_HEADS, PEER_NKEYS // SUBLANES, SUBLANES, tt)
    return pl.pallas_call(
        functools.partial(_peer_kernel, pairs_per_tile=pairs_per_tile, n_pairs=n_pairs),
        grid=(n_pairs + 1,),
        in_specs=[pl.BlockSpec((tt, d), lambda g: (cur_pair(g) // pairs_per_tile, 0)),
                  pl.BlockSpec((tt, d), lambda g: (done_tile(g), 0)),
                  _const_spec(nfw.shape), _const_spec(wq_t.shape), _const_spec(k1.shape),
                  _const_spec(k2.shape),
                  pl.BlockSpec((2 * ec, d), lambda g: (cur_pair(g) % pairs_per_tile, 0)),
                  pl.BlockSpec((d, 2 * ec), lambda g: (0, jnp.maximum(g - 1, 0) % pairs_per_tile)),
                  _const_spec(nlw.shape)],
        out_specs=pl.BlockSpec((tt, d), lambda g: (done_tile(g), 0)),
        out_shape=jax.ShapeDtypeStruct((t, d), F32),
        scratch_shapes=[pltpu.VMEM((d, tt), BF16),
                        pltpu.VMEM((d, tt), F32),
                        pltpu.VMEM((ec, tt), F32), pltpu.VMEM((ec, tt), F32),
                        pltpu.VMEM((ec, tt), BF16), pltpu.VMEM((ec, tt), BF16),
                        pltpu.VMEM(row_shape, F32), pltpu.VMEM(gate_shape, F32),
                        pltpu.VMEM((PEER_HEADS, SUBLANES, tt), F32)],
        compiler_params=pltpu.CompilerParams(
            dimension_semantics=("arbitrary",), vmem_limit_bytes=VMEM_LIMIT_BYTES),
        name="peer",
    )(h, h, nfw, wq_t, k1, k2, wd, wu_t, nlw)


def _pad_lanes(a, width):
    return jnp.pad(a, [(0, 0)] * (a.ndim - 1) + [(0, width - a.shape[-1])])


def kernel(x, meta_tokens, norm_mix_w, w_in, ssd_conv_w, ssd_conv_b, ssd_dt_bias, ssd_A_log, ssd_D,
           ssd_norm_w, conf_conv_w, conf_conv_b, conf_ln_g, conf_ln_b, w_out, norm_ffn_w,
           peer_w_query, peer_sub_keys_1, peer_sub_keys_2, peer_w_down, peer_w_up, norm_final_w):
    b, seq, d = x.shape
    assert norm_mix_w.shape[0] == 1, "one layer"
    d_xbc = d + 2 * SSD_GROUPS * SSD_STATE
    n_heads = d // SSD_HEADDIM
    row = lambda a: a.reshape(1, -1).astype(F32)

    wi = w_in[0]
    o1, o2, o3 = d, d + d_xbc, d + d_xbc + n_heads
    w_in_r = jnp.concatenate([wi[:, :o1], wi[:, o1:o2], wi[:, o3:], _pad_lanes(wi[:, o2:o3], DT_PAD)],
                             axis=1).astype(BF16)
    wts = {
        "norm_mix_w": row(norm_mix_w[0]), "w_in": w_in_r,
        "ssd_conv_w": ssd_conv_w[0].astype(F32), "ssd_conv_b": row(ssd_conv_b[0]),
        "ssd_dt_bias": _pad_lanes(row(ssd_dt_bias[0]), DT_PAD),
        "ssd_a_log": _pad_lanes(row(ssd_A_log[0]), DT_PAD),
        "ssd_d": row(jnp.repeat(ssd_D[0], SSD_HEADDIM)), "ssd_norm_w": row(ssd_norm_w[0]),
        "conf_conv_w": conf_conv_w[0].astype(F32), "conf_conv_b": row(conf_conv_b[0]),
        "conf_ln_g": row(conf_ln_g[0]), "conf_ln_b": row(conf_ln_b[0]),
        "w_out": w_out[0].astype(BF16),
    }

    pad = SSD_SUB - N_META
    meta_tile = jnp.pad(meta_tokens.astype(F32), [(pad, 0), (0, 0)])
    keep = jnp.pad(jnp.ones((N_META, DT_PAD), F32), [(pad, 0), (0, 0)])
    h1 = _mixer_calls(x.astype(F32), meta_tile, keep, wts)

    out = _peer_call(
        h1.reshape(b * seq, d), row(norm_ffn_w[0]),
        peer_w_query[0].T.astype(BF16), peer_sub_keys_1[0].astype(BF16), peer_sub_keys_2[0].astype(BF16),
        peer_w_down[0].astype(BF16), peer_w_up[0].T.astype(BF16), row(norm_final_w))
    return out.reshape(b, seq, d).astype(x.dtype)
```

```python
import functools
import math

import jax
import jax.numpy as jnp
from jax import lax
from jax.experimental import pallas as pl
from jax.experimental.pallas import tpu as pltpu

F32 = jnp.float32
BF16 = jnp.bfloat16

EPS = 1e-5
N_META = 16
SSD_HEADDIM = 64
SSD_GROUPS = 4
SSD_STATE = 128
SSD_CONV = 4
CONF_KERNEL = 31
PEER_HEADS = 8
PEER_NKEYS = 128
PEER_TOPK = 16
LOG2E = math.log2(math.e)

LANES = 128
SUBLANES = 8
VMEM_LIMIT_BYTES = 56 * 1024 * 1024

SSD_SUB = 128
XBC_HALO = 8
GLU_HALO = 32
DT_PAD = LANES

MIX_TILE = 512
CONV_ROWS = 64
PEER_TOK = 512
PEER_EXP = SUBLANES * PEER_NKEYS
GATE_ROWS = 64


def _dot(a, b):
    return jnp.dot(a, b, preferred_element_type=F32)


def _dot_nt(a, b):
    return lax.dot_general(a, b, (((1,), (1,)), ((), ())), preferred_element_type=F32)


def _dot_tn(a, b):
    return lax.dot_general(a, b, (((0,), (0,)), ((), ())), preferred_element_type=F32)


def _rmsnorm(x, w):
    return x * lax.rsqrt(jnp.mean(x * x, axis=-1, keepdims=True) + EPS) * w


def _silu(x):
    return x * jax.nn.sigmoid(x)


def _softplus(x):
    return jnp.maximum(x, 0.0) + jnp.log1p(jnp.exp(-jnp.abs(x)))


def _split3(x):
    hi = x.astype(BF16)
    r1 = x - hi.astype(F32)
    mid = r1.astype(BF16)
    lo = (r1 - mid.astype(F32)).astype(BF16)
    return hi, mid, lo


def _expand_heads(a, n_heads):
    q = a.shape[0]
    lane = lax.broadcasted_iota(jnp.int32, (q, LANES), 1)
    blocks = []
    for j in range(n_heads // 2):
        blocks.append(jnp.where(lane < SSD_HEADDIM, a[:, 2 * j:2 * j + 1], a[:, 2 * j + 1:2 * j + 2]))
    return jnp.concatenate(blocks, axis=1)


def _mixer_tile(x, dt_keep, w, state_ref, xbc_ext_ref, glu_ext_ref, conv_scratch=None):
    with_output = conv_scratch is not None
    L, d_model = x.shape
    d_ssd = d_model
    n_heads = d_ssd // SSD_HEADDIM
    d_bc = SSD_GROUPS * SSD_STATE
    d_xbc = d_ssd + 2 * d_bc
    hpg = n_heads // SSD_GROUPS
    gw = hpg * SSD_HEADDIM

    u = _rmsnorm(x, w["norm_mix_w"][...]).astype(BF16)
    o_xbc = d_ssd
    o_conf = o_xbc + d_xbc
    o_dt = o_conf + 2 * d_model
    w_in = w["w_in"]
    xbc_pre = _dot(u, w_in[:, o_xbc:o_conf])
    dt_raw = _dot(u, w_in[:, o_dt:o_dt + DT_PAD])

    xbc_ext_ref[XBC_HALO:XBC_HALO + L, :] = xbc_pre
    acc = jnp.broadcast_to(w["ssd_conv_b"][...], (L, d_xbc))
    for k in range(SSD_CONV):
        off = XBC_HALO - (SSD_CONV - 1) + k
        acc = acc + xbc_ext_ref[pl.ds(off, L), :] * w["ssd_conv_w"][k:k + 1, :]
    xbc_ext_ref[0:XBC_HALO, :] = xbc_ext_ref[L:L + XBC_HALO, :]
    xbc = _silu(acc)

    dt = _softplus(dt_raw + w["ssd_dt_bias"][...])
    if dt_keep is not None:
        dt = dt * dt_keep
    a_neg = -jnp.exp(w["ssd_a_log"][...])
    dta = dt * a_neg

    row = lax.broadcasted_iota(jnp.int32, (SSD_SUB, SSD_SUB), 0)
    col = lax.broadcasted_iota(jnp.int32, (SSD_SUB, SSD_SUB), 1)
    causal = row >= col
    tri = jnp.where(causal, 1.0, 0.0).astype(BF16)
    lane = lax.broadcasted_iota(jnp.int32, (SSD_SUB, LANES), 1)

    y_rows = []
    for j in range(L // SSD_SUB):
        r0 = j * SSD_SUB
        xs = xbc[r0:r0 + SSD_SUB, 0:d_ssd]
        bm = xbc[r0:r0 + SSD_SUB, d_ssd:d_ssd + d_bc].astype(BF16)
        cm = xbc[r0:r0 + SSD_SUB, d_ssd + d_bc:d_xbc].astype(BF16)
        dt_j = dt[r0:r0 + SSD_SUB]
        hi, mid, lo = _split3(dta[r0:r0 + SSD_SUB])
        a_cs = _dot(tri, hi) + _dot(tri, mid) + _dot(tri, lo)
        a_exp = _expand_heads(a_cs, n_heads)
        dt_exp = _expand_heads(dt_j, n_heads)
        a_last = a_exp[SSD_SUB - 1:SSD_SUB, :]
        xdt = xs * dt_exp
        xds = (xdt * jnp.exp(a_last - a_exp)).astype(BF16)
        chunk_decay = jnp.exp(a_last)
        if with_output:
            a_cs_t = a_cs.T
            ea = jnp.exp(a_exp)
            xdt_b = xdt.astype(BF16)
        y_cols = []
        for g in range(SSD_GROUPS):
            bm_g = bm[:, g * SSD_STATE:(g + 1) * SSD_STATE]
            cm_g = cm[:, g * SSD_STATE:(g + 1) * SSD_STATE]
            st_prev = state_ref[g]
            new_st = _dot_tn(bm_g, xds[:, g * gw:(g + 1) * gw])
            state_ref[g] = st_prev * chunk_decay[:, g * gw:(g + 1) * gw] + new_st
            if not with_output:
                continue
            y_off = _dot(cm_g, st_prev.astype(BF16)) * ea[:, g * gw:(g + 1) * gw]
            cb = _dot_nt(cm_g, bm_g)
            for pr in range(hpg // 2):
                ys = []
                for hh in range(2):
                    hd = g * hpg + 2 * pr + hh
                    seg = a_cs[:, hd:hd + 1] - a_cs_t[hd:hd + 1, :]
                    lmat = jnp.where(causal, jnp.exp(jnp.minimum(seg, 0.0)), 0.0)
                    xp = xdt_b[:, (g * hpg + 2 * pr) * SSD_HEADDIM:(g * hpg + 2 * pr + 2) * SSD_HEADDIM]
                    ys.append(_dot((cb * lmat).astype(BF16), xp))
                yd = jnp.where(lane < SSD_HEADDIM, ys[0], ys[1])
                y_cols.append(yd + y_off[:, pr * LANES:(pr + 1) * LANES])
        if with_output:
            y_rows.append(jnp.concatenate(y_cols, axis=1) + w["ssd_d"][...] * xs)

    conf = _dot(u, w_in[:, o_conf:o_dt])
    glu = conf[:, :d_model] * jax.nn.sigmoid(conf[:, d_model:])
    for lt in range(d_model // LANES):
        glu_ext_ref[lt, GLU_HALO:GLU_HALO + L, :] = glu[:, lt * LANES:(lt + 1) * LANES]
    if not with_output:
        glu_ext_ref[:, 0:GLU_HALO, :] = glu_ext_ref[:, L:L + GLU_HALO, :]
        return None

    z = _dot(u, w_in[:, 0:d_ssd])
    y = jnp.concatenate(y_rows, axis=0) if len(y_rows) > 1 else y_rows[0]
    y = y * _silu(z)
    parts = []
    for g in range(SSD_GROUPS):
        yg = y[:, g * gw:(g + 1) * gw]
        parts.append(yg * lax.rsqrt(jnp.mean(yg * yg, axis=-1, keepdims=True) + EPS))
    y_ssd = jnp.concatenate(parts, axis=1) * w["ssd_norm_w"][...]

    shift_ref, hc_ref = conv_scratch
    n_sh = shift_ref.shape[2]
    for r in range(1, SUBLANES):
        shift_ref[r - 1] = glu_ext_ref[:, pl.ds(r, n_sh), :]
    for lt in range(d_model // LANES):
        ls = pl.ds(lt * LANES, LANES)
        for rb in range(L // CONV_ROWS):
            acc = jnp.broadcast_to(w["conf_conv_b"][:, ls], (CONV_ROWS, LANES))
            for k in range(CONF_KERNEL):
                off = GLU_HALO - (CONF_KERNEL - 1) + k
                r = off % SUBLANES
                src = glu_ext_ref.at[lt] if r == 0 else shift_ref.at[r - 1, lt]
                acc = acc + src[pl.ds(off - r + rb * CONV_ROWS, CONV_ROWS), :] * w["conf_conv_w"][k:k + 1, ls]
            hc_ref[pl.ds(rb * CONV_ROWS, CONV_ROWS), ls] = acc
    glu_ext_ref[:, 0:GLU_HALO, :] = glu_ext_ref[:, L:L + GLU_HALO, :]
    hc = hc_ref[...]
    mu = jnp.mean(hc, axis=-1, keepdims=True)
    hcc = hc - mu
    var = jnp.mean(hcc * hcc, axis=-1, keepdims=True)
    hn = hcc * lax.rsqrt(var + EPS) * w["conf_ln_g"][...] + w["conf_ln_b"][...]
    y_conf = _silu(hn)

    mix = jnp.concatenate([y_ssd, y_conf], axis=1).astype(BF16)
    return x + _dot(mix, w["w_out"][...])


_MIX_WEIGHTS = ("norm_mix_w", "w_in", "ssd_conv_w", "ssd_conv_b", "ssd_dt_bias", "ssd_a_log",
                "ssd_d", "ssd_norm_w", "conf_conv_w", "conf_conv_b", "conf_ln_g", "conf_ln_b", "w_out")


def _meta_kernel(x_ref, keep_ref, *refs):
    nw = len(_MIX_WEIGHTS)
    w = dict(zip(_MIX_WEIGHTS, refs[:nw]))
    state_ref, xbc_halo_ref, glu_halo_ref, xbc_ext_ref, glu_ext_ref = refs[nw:]
    state_ref[...] = jnp.zeros(state_ref.shape, F32)
    xbc_ext_ref[0:XBC_HALO, :] = jnp.zeros((XBC_HALO, xbc_ext_ref.shape[1]), F32)
    glu_ext_ref[:, 0:GLU_HALO, :] = jnp.zeros(glu_halo_ref.shape, F32)
    _mixer_tile(x_ref[...], keep_ref[...], w, state_ref, xbc_ext_ref, glu_ext_ref)
    xbc_halo_ref[...] = xbc_ext_ref[0:XBC_HALO, :]
    glu_halo_ref[...] = glu_ext_ref[:, 0:GLU_HALO, :]


def _mixer_kernel(x_ref, state0_ref, xbc_halo0_ref, glu_halo0_ref, *refs):
    nw = len(_MIX_WEIGHTS)
    w = dict(zip(_MIX_WEIGHTS, refs[:nw]))
    out_ref, state_ref, xbc_ext_ref, glu_ext_ref, shift_ref, hc_ref = refs[nw:]

    @pl.when(pl.program_id(1) == 0)
    def _():
        state_ref[...] = state0_ref[...]
        xbc_ext_ref[0:XBC_HALO, :] = xbc_halo0_ref[...]
        glu_ext_ref[:, 0:GLU_HALO, :] = glu_halo0_ref[...]

    out_ref[...] = _mixer_tile(x_ref[...], None, w, state_ref, xbc_ext_ref, glu_ext_ref,
                               (shift_ref, hc_ref))


def _const_spec(shape):
    nd = len(shape)
    return pl.BlockSpec(shape, lambda *_: (0,) * nd, pipeline_mode=pl.Buffered(1))


def _mixer_calls(x, meta_tile, keep, wts):
    b, seq, d = x.shape
    n_heads = d // SSD_HEADDIM
    gw = (n_heads // SSD_GROUPS) * SSD_HEADDIM
    d_xbc = d + 2 * SSD_GROUPS * SSD_STATE
    w_list = [wts[k] for k in _MIX_WEIGHTS]
    state_shape = (SSD_GROUPS, SSD_STATE, gw)
    glu_halo_shape = (d // LANES, GLU_HALO, LANES)

    lm = meta_tile.shape[0]
    state0, xbc_halo0, glu_halo0 = pl.pallas_call(
        _meta_kernel,
        out_shape=(jax.ShapeDtypeStruct(state_shape, F32),
                   jax.ShapeDtypeStruct((XBC_HALO, d_xbc), F32),
                   jax.ShapeDtypeStruct(glu_halo_shape, F32)),
        scratch_shapes=[pltpu.VMEM((XBC_HALO + lm, d_xbc), F32),
                        pltpu.VMEM((d // LANES, GLU_HALO + lm, LANES), F32)],
        compiler_params=pltpu.CompilerParams(vmem_limit_bytes=VMEM_LIMIT_BYTES),
        name="meta_prologue",
    )(meta_tile, keep, *w_list)

    lt = MIX_TILE
    assert seq % lt == 0
    return pl.pallas_call(
        _mixer_kernel,
        grid=(b, seq // lt),
        in_specs=[pl.BlockSpec((None, lt, d), lambda i, c: (i, c, 0)),
                  _const_spec(state_shape), _const_spec((XBC_HALO, d_xbc)), _const_spec(glu_halo_shape)]
                 + [_const_spec(a.shape) for a in w_list],
        out_specs=pl.BlockSpec((None, lt, d), lambda i, c: (i, c, 0)),
        out_shape=jax.ShapeDtypeStruct((b, seq, d), F32),
        scratch_shapes=[pltpu.VMEM(state_shape, F32),
                        pltpu.VMEM((XBC_HALO + lt, d_xbc), F32),
                        pltpu.VMEM((d // LANES, GLU_HALO + lt, LANES), F32),
                        pltpu.VMEM((SUBLANES - 1, d // LANES, GLU_HALO + lt - SUBLANES, LANES), F32),
                        pltpu.VMEM((lt, d), F32)],
        compiler_params=pltpu.CompilerParams(
            dimension_semantics=("arbitrary", "arbitrary"), vmem_limit_bytes=VMEM_LIMIT_BYTES),
        name="mixer",
    )(x, state0, xbc_halo0, glu_halo0, *w_list)


def _sort_desc(v):
    v = list(v)
    n = len(v)
    k = 2
    while k <= n:
        j = k // 2
        while j >= 1:
            for i in range(n):
                l = i ^ j
                if l > i:
                    hi, lo = jnp.maximum(v[i], v[l]), jnp.minimum(v[i], v[l])
                    v[i], v[l] = (hi, lo) if (i & k) == 0 else (lo, hi)
            j //= 2
        k *= 2
    return v


def _merge_top(a, b, *, sort=True):
    n = len(a)
    v = [jnp.maximum(a[i], b[n - 1 - i]) for i in range(n)]
    if not sort:
        return v
    j = n // 2
    while j >= 1:
        for i in range(n):
            l = i ^ j
            if l > i:
                v[i], v[l] = jnp.maximum(v[i], v[l]), jnp.minimum(v[i], v[l])
        j //= 2
    return v


def _top_sorted(s):
    n = s.shape[0] // SUBLANES
    v = _sort_desc([s[SUBLANES * i:SUBLANES * (i + 1), :] for i in range(n)])
    for shift in (4, 2, 1):
        v = _merge_top(v, [pltpu.roll(a, shift, axis=0) for a in v])
    return v


def _top_pair_sums(v1, v2):
    t = v1[0].shape[1]
    sub = lax.broadcasted_iota(jnp.int32, (SUBLANES, t), 0)
    lists = []
    for q in range(PEER_TOPK // SUBLANES):
        col = v1[SUBLANES * q]
        for s in range(1, SUBLANES):
            col = jnp.where(sub == s, v1[SUBLANES * q + s], col)
        lists.append([col + b for b in v2])
    m = _merge_top(lists[0], lists[1])
    m = _merge_top(m, [pltpu.roll(a, 4, axis=0) for a in m])
    m = _merge_top(m, [pltpu.roll(a, 2, axis=0) for a in m])
    return _merge_top(m, [pltpu.roll(a, 1, axis=0) for a in m], sort=False)


def _peer_scores(xt_ref, wq_ref, k1_ref, k2_ref, s1_ref, s2_ref, tau_ref):
    tt = xt_ref.shape[1]
    dk = k1_ref.shape[2]
    qt = _dot(wq_ref[...], xt_ref[...]).astype(BF16)
    for h in range(PEER_HEADS):
        s1 = _dot(k1_ref[h], qt[2 * dk * h:2 * dk * h + dk, :]) * LOG2E
        s2 = _dot(k2_ref[h], qt[2 * dk * h + dk:2 * dk * (h + 1), :]) * LOG2E
        for lt in range(tt // LANES):
            ls = slice(lt * LANES, (lt + 1) * LANES)
            v1 = _top_sorted(s1[:, ls])
            v2 = _top_sorted(s2[:, ls])
            top = v1[0] + v2[0]
            z = functools.reduce(lambda a, b: a + b, [jnp.exp2(x - top) for x in _top_pair_sums(v1, v2)])
            c = jnp.broadcast_to((top + jnp.log2(z) + 1.0)[0:1, :], top.shape)
            tau_ref[h, :, ls] = functools.reduce(jnp.minimum, _top_pair_sums([v - c for v in v1], v2))
            s1y = s1[:, ls] - c[0:1, :]
            for blk in range(PEER_NKEYS // SUBLANES):
                s1_ref[h, blk, :, ls] = s1y[blk * SUBLANES:(blk + 1) * SUBLANES, :]
            s2_ref[h, lt] = s2[:, ls]


def _peer_activations(lt, chunk, s_ref, a_ref, s1_ref, s2_ref, tau_ref):
    ls = pl.ds(lt * LANES, LANES)
    bshape = (GATE_ROWS, LANES)
    tau = [jnp.broadcast_to(tau_ref[h, 0:1, ls], bshape) for h in range(PEER_HEADS)]
    for k in range(SUBLANES):
        s1b = [jnp.broadcast_to(s1_ref[h, chunk, k:k + 1, ls], bshape) for h in range(PEER_HEADS)]
        for blk in range(PEER_NKEYS // GATE_ROWS):
            rs = pl.ds(blk * GATE_ROWS, GATE_ROWS)
            gate = None
            for h in range(PEER_HEADS):
                y = s1b[h] + s2_ref[h, lt, rs, :]
                term = jnp.where(y >= tau[h], jnp.exp2(y), 0.0)
                gate = term if gate is None else gate + term
            rows = pl.ds(k * PEER_NKEYS + blk * GATE_ROWS, GATE_ROWS)
            pre = s_ref[rows, ls]
            a_ref[rows, ls] = (pre * (1.0 + lax.erf(pre * math.sqrt(0.5))) * gate).astype(BF16)


def _peer_kernel(hcur_ref, hres_ref, nfw_ref, wq_ref, k1_ref, k2_ref, wd_ref, wu_ref, nlw_ref, out_ref,
                 xt_ref, acc_ref, sa_ref, sb_ref, aa_ref, ab_ref, s1_ref, s2_ref, tau_ref,
                 *, pairs_per_tile, n_pairs):
    g = pl.program_id(0)
    tt, d = hcur_ref.shape
    ec = wd_ref.shape[0] // 2
    assert ec == SUBLANES * PEER_NKEYS
    tile_start = g % pairs_per_tile == 0
    in_range = g < n_pairs

    @pl.when(g == 0)
    def _():
        for ref in (sb_ref, aa_ref, acc_ref, s1_ref, s2_ref, tau_ref):
            ref[...] = jnp.zeros(ref.shape, ref.dtype)

    @pl.when(jnp.logical_and(tile_start, in_range))
    def _():
        xn = _rmsnorm(hcur_ref[...], nfw_ref[...])
        xt_ref[...] = xn.T.astype(BF16)

    def step(sub, s_new, s_old, a_new, a_old, chunk_old):
        wd = wd_ref.at[pl.ds(sub * ec, ec), :]
        wu = wu_ref.at[:, pl.ds(sub * ec, ec)]
        s_new[...] = _dot(wd[...], xt_ref[...])
        acc_ref[...] += _dot(wu[...], a_old[...])
        for lt in range(tt // LANES):
            _peer_activations(lt, chunk_old, s_old, a_new, s1_ref, s2_ref, tau_ref)

    chunks = 2 * pairs_per_tile
    step(0, sa_ref, sb_ref, ab_ref, aa_ref, jnp.maximum(2 * g - 1, 0) % chunks)

    @pl.when(jnp.logical_and(tile_start, in_range))
    def _():
        _peer_scores(xt_ref, wq_ref, k1_ref, k2_ref, s1_ref, s2_ref, tau_ref)

    step(1, sb_ref, sa_ref, aa_ref, ab_ref, (2 * g) % chunks)

    @pl.when(jnp.logical_and(tile_start, g > 0))
    def _():
        out_ref[...] = _rmsnorm(hres_ref[...] + acc_ref[...].T, nlw_ref[...])
        acc_ref[...] = jnp.zeros(acc_ref.shape, F32)


def _peer_call(h, nfw, wq_t, k1, k2, wd, wu_t, nlw):
    t, d = h.shape
    n_exp = wd.shape[0]
    tt, ec = PEER_TOK, PEER_EXP
    assert t % tt == 0 and n_exp % (2 * ec) == 0
    pairs_per_tile = n_exp // (2 * ec)
    n_pairs = (t // tt) * pairs_per_tile
    cur_pair = lambda g: jnp.minimum(g, n_pairs - 1)
    done_tile = lambda g: jnp.maximum(2 * g - 1, 0) // (2 * pairs_per_tile)
    gate_shape = (PEER_HEADS, tt // LANES, PEER_NKEYS, LANES)
    row_shape = (PEER_HEADS, PEER_NKEYS // SUBLANES, SUBLANES, tt)
    return pl.pallas_call(
        functools.partial(_peer_kernel, pairs_per_tile=pairs_per_tile, n_pairs=n_pairs),
        grid=(n_pairs + 1,),
        in_specs=[pl.BlockSpec((tt, d), lambda g: (cur_pair(g) // pairs_per_tile, 0)),
                  pl.BlockSpec((tt, d), lambda g: (done_tile(g), 0)),
                  _const_spec(nfw.shape), _const_spec(wq_t.shape), _const_spec(k1.shape),
                  _const_spec(k2.shape),
                  pl.BlockSpec((2 * ec, d), lambda g: (cur_pair(g) % pairs_per_tile, 0)),
                  pl.BlockSpec((d, 2 * ec), lambda g: (0, jnp.maximum(g - 1, 0) % pairs_per_tile)),
                  _const_spec(nlw.shape)],
        out_specs=pl.BlockSpec((tt, d), lambda g: (done_tile(g), 0)),
        out_shape=jax.ShapeDtypeStruct((t, d), F32),
        scratch_shapes=[pltpu.VMEM((d, tt), BF16),
                        pltpu.VMEM((d, tt), F32),
                        pltpu.VMEM((ec, tt), F32), pltpu.VMEM((ec, tt), F32),
                        pltpu.VMEM((ec, tt), BF16), pltpu.VMEM((ec, tt), BF16),
                        pltpu.VMEM(row_shape, F32), pltpu.VMEM(gate_shape, F32),
                        pltpu.VMEM((PEER_HEADS, SUBLANES, tt), F32)],
        compiler_params=pltpu.CompilerParams(
            dimension_semantics=("arbitrary",), vmem_limit_bytes=VMEM_LIMIT_BYTES),
        name="peer",
    )(h, h, nfw, wq_t, k1, k2, wd, wu_t, nlw)


def _pad_lanes(a, width):
    return jnp.pad(a, [(0, 0)] * (a.ndim - 1) + [(0, width - a.shape[-1])])


def kernel(x, meta_tokens, norm_mix_w, w_in, ssd_conv_w, ssd_conv_b, ssd_dt_bias, ssd_A_log, ssd_D,
           ssd_norm_w, conf_conv_w, conf_conv_b, conf_ln_g, conf_ln_b, w_out, norm_ffn_w,
           peer_w_query, peer_sub_keys_1, peer_sub_keys_2, peer_w_down, peer_w_up, norm_final_w):
    b, seq, d = x.shape
    assert norm_mix_w.shape[0] == 1, "one layer"
    d_xbc = d + 2 * SSD_GROUPS * SSD_STATE
    n_heads = d // SSD_HEADDIM
    row = lambda a: a.reshape(1, -1).astype(F32)

    wi = w_in[0]
    o1, o2, o3 = d, d + d_xbc, d + d_xbc + n_heads
    w_in_r = jnp.concatenate([wi[:, :o1], wi[:, o1:o2], wi[:, o3:], _pad_lanes(wi[:, o2:o3], DT_PAD)],
                             axis=1).astype(BF16)
    wts = {
        "norm_mix_w": row(norm_mix_w[0]), "w_in": w_in_r,
        "ssd_conv_w": ssd_conv_w[0].astype(F32), "ssd_conv_b": row(ssd_conv_b[0]),
        "ssd_dt_bias": _pad_lanes(row(ssd_dt_bias[0]), DT_PAD),
        "ssd_a_log": _pad_lanes(row(ssd_A_log[0]), DT_PAD),
        "ssd_d": row(jnp.repeat(ssd_D[0], SSD_HEADDIM)), "ssd_norm_w": row(ssd_norm_w[0]),
        "conf_conv_w": conf_conv_w[0].astype(F32), "conf_conv_b": row(conf_conv_b[0]),
        "conf_ln_g": row(conf_ln_g[0]), "conf_ln_b": row(conf_ln_b[0]),
        "w_out": w_out[0].astype(BF16),
    }

    pad = SSD_SUB - N_META
    meta_tile = jnp.pad(meta_tokens.astype(F32), [(pad, 0), (0, 0)])
    keep = jnp.pad(jnp.ones((N_META, DT_PAD), F32), [(pad, 0), (0, 0)])
    h1 = _mixer_calls(x.astype(F32), meta_tile, keep, wts)

    out = _peer_call(
        h1.reshape(b * seq, d), row(norm_ffn_w[0]),
        peer_w_query[0].T.astype(BF16), peer_sub_keys_1[0].astype(BF16), peer_sub_keys_2[0].astype(BF16),
        peer_w_down[0].astype(BF16), peer_w_up[0].T.astype(BF16), row(norm_final_w))
    return out.reshape(b, seq, d).astype(x.dtype)
```

```python
import functools
import math

import jax
import jax.numpy as jnp
from jax import lax
from jax.experimental import pallas as pl
from jax.experimental.pallas import tpu as pltpu

F32 = jnp.float32
BF16 = jnp.bfloat16

EPS = 1e-5
N_META = 16
SSD_HEADDIM = 64
SSD_GROUPS = 4
SSD_STATE = 128
SSD_CONV = 4
CONF_KERNEL = 31
PEER_HEADS = 8
PEER_NKEYS = 128
PEER_TOPK = 16
LOG2E = math.log2(math.e)

LANES = 128
SUBLANES = 8
VMEM_LIMIT_BYTES = 56 * 1024 * 1024

SSD_SUB = 128
XBC_HALO = 8
GLU_HALO = 32
DT_PAD = LANES

MIX_TILE = 512
CONV_ROWS = 64
PEER_TOK = 512
PEER_EXP = SUBLANES * PEER_NKEYS
GATE_ROWS = 64


def _dot(a, b):
    return jnp.dot(a, b, preferred_element_type=F32)


def _dot_nt(a, b):
    return lax.dot_general(a, b, (((1,), (1,)), ((), ())), preferred_element_type=F32)


def _dot_tn(a, b):
    return lax.dot_general(a, b, (((0,), (0,)), ((), ())), preferred_element_type=F32)


def _rmsnorm(x, w):
    return x * lax.rsqrt(jnp.mean(x * x, axis=-1, keepdims=True) + EPS) * w


def _silu(x):
    return x * jax.nn.sigmoid(x)


def _softplus(x):
    return jnp.maximum(x, 0.0) + jnp.log1p(jnp.exp(-jnp.abs(x)))


def _split3(x):
    hi = x.astype(BF16)
    r1 = x - hi.astype(F32)
    mid = r1.astype(BF16)
    lo = (r1 - mid.astype(F32)).astype(BF16)
    return hi, mid, lo


def _expand_heads(a, n_heads):
    q = a.shape[0]
    lane = lax.broadcasted_iota(jnp.int32, (q, LANES), 1)
    blocks = []
    for j in range(n_heads // 2):
        blocks.append(jnp.where(lane < SSD_HEADDIM, a[:, 2 * j:2 * j + 1], a[:, 2 * j + 1:2 * j + 2]))
    return jnp.concatenate(blocks, axis=1)


def _mixer_tile(x, dt_keep, w, state_ref, xbc_ext_ref, glu_ext_ref, conv_scratch=None):
    with_output = conv_scratch is not None
    L, d_model = x.shape
    d_ssd = d_model
    n_heads = d_ssd // SSD_HEADDIM
    d_bc = SSD_GROUPS * SSD_STATE
    d_xbc = d_ssd + 2 * d_bc
    hpg = n_heads // SSD_GROUPS
    gw = hpg * SSD_HEADDIM

    u = _rmsnorm(x, w["norm_mix_w"][...]).astype(BF16)
    o_xbc = d_ssd
    o_conf = o_xbc + d_xbc
    o_dt = o_conf + 2 * d_model
    w_in = w["w_in"]
    xbc_pre = _dot(u, w_in[:, o_xbc:o_conf])
    dt_raw = _dot(u, w_in[:, o_dt:o_dt + DT_PAD])

    xbc_ext_ref[XBC_HALO:XBC_HALO + L, :] = xbc_pre
    acc = jnp.broadcast_to(w["ssd_conv_b"][...], (L, d_xbc))
    for k in range(SSD_CONV):
        off = XBC_HALO - (SSD_CONV - 1) + k
        acc = acc + xbc_ext_ref[pl.ds(off, L), :] * w["ssd_conv_w"][k:k + 1, :]
    xbc_ext_ref[0:XBC_HALO, :] = xbc_ext_ref[L:L + XBC_HALO, :]
    xbc = _silu(acc)

    dt = _softplus(dt_raw + w["ssd_dt_bias"][...])
    if dt_keep is not None:
        dt = dt * dt_keep
    a_neg = -jnp.exp(w["ssd_a_log"][...])
    dta = dt * a_neg

    row = lax.broadcasted_iota(jnp.int32, (SSD_SUB, SSD_SUB), 0)
    col = lax.broadcasted_iota(jnp.int32, (SSD_SUB, SSD_SUB), 1)
    causal = row >= col
    tri = jnp.where(causal, 1.0, 0.0).astype(BF16)
    lane = lax.broadcasted_iota(jnp.int32, (SSD_SUB, LANES), 1)

    y_rows = []
    for j in range(L // SSD_SUB):
        r0 = j * SSD_SUB
        xs = xbc[r0:r0 + SSD_SUB, 0:d_ssd]
        bm = xbc[r0:r0 + SSD_SUB, d_ssd:d_ssd + d_bc].astype(BF16)
        cm = xbc[r0:r0 + SSD_SUB, d_ssd + d_bc:d_xbc].astype(BF16)
        dt_j = dt[r0:r0 + SSD_SUB]
        hi, mid, lo = _split3(dta[r0:r0 + SSD_SUB])
        a_cs = _dot(tri, hi) + _dot(tri, mid) + _dot(tri, lo)
        a_exp = _expand_heads(a_cs, n_heads)
        dt_exp = _expand_heads(dt_j, n_heads)
        a_last = a_exp[SSD_SUB - 1:SSD_SUB, :]
        xdt = xs * dt_exp
        xds = (xdt * jnp.exp(a_last - a_exp)).astype(BF16)
        chunk_decay = jnp.exp(a_last)
        if with_output:
            a_cs_t = a_cs.T
            ea = jnp.exp(a_exp)
            xdt_b = xdt.astype(BF16)
        y_cols = []
        for g in range(SSD_GROUPS):
            bm_g = bm[:, g * SSD_STATE:(g + 1) * SSD_STATE]
            cm_g = cm[:, g * SSD_STATE:(g + 1) * SSD_STATE]
            st_prev = state_ref[g]
            new_st = _dot_tn(bm_g, xds[:, g * gw:(g + 1) * gw])
            state_ref[g] = st_prev * chunk_decay[:, g * gw:(g + 1) * gw] + new_st
            if not with_output:
                continue
            y_off = _dot(cm_g, st_prev.astype(BF16)) * ea[:, g * gw:(g + 1) * gw]
            cb = _dot_nt(cm_g, bm_g)
            for pr in range(hpg // 2):
                ys = []
                for hh in range(2):
                    hd = g * hpg + 2 * pr + hh
                    seg = a_cs[:, hd:hd + 1] - a_cs_t[hd:hd + 1, :]
                    lmat = jnp.where(causal, jnp.exp(jnp.minimum(seg, 0.0)), 0.0)
                    xp = xdt_b[:, (g * hpg + 2 * pr) * SSD_HEADDIM:(g * hpg + 2 * pr + 2) * SSD_HEADDIM]
                    ys.append(_dot((cb * lmat).astype(BF16), xp))
                yd = jnp.where(lane < SSD_HEADDIM, ys[0], ys[1])
                y_cols.append(yd + y_off[:, pr * LANES:(pr + 1) * LANES])
        if with_output:
            y_rows.append(jnp.concatenate(y_cols, axis=1) + w["ssd_d"][...] * xs)

    conf = _dot(u, w_in[:, o_conf:o_dt])
    glu = conf[:, :d_model] * jax.nn.sigmoid(conf[:, d_model:])
    for lt in range(d_model // LANES):
        glu_ext_ref[lt, GLU_HALO:GLU_HALO + L, :] = glu[:, lt * LANES:(lt + 1) * LANES]
    if not with_output:
        glu_ext_ref[:, 0:GLU_HALO, :] = glu_ext_ref[:, L:L + GLU_HALO, :]
        return None

    z = _dot(u, w_in[:, 0:d_ssd])
    y = jnp.concatenate(y_rows, axis=0) if len(y_rows) > 1 else y_rows[0]
    y = y * _silu(z)
    parts = []
    for g in range(SSD_GROUPS):
        yg = y[:, g * gw:(g + 1) * gw]
        parts.append(yg * lax.rsqrt(jnp.mean(yg * yg, axis=-1, keepdims=True) + EPS))
    y_ssd = jnp.concatenate(parts, axis=1) * w["ssd_norm_w"][...]

    shift_ref, hc_ref = conv_scratch
    n_sh = shift_ref.shape[2]
    for r in range(1, SUBLANES):
        shift_ref[r - 1] = glu_ext_ref[:, pl.ds(r, n_sh), :]
    for lt in range(d_model // LANES):
        ls = pl.ds(lt * LANES, LANES)
        for rb in range(L // CONV_ROWS):
            acc = jnp.broadcast_to(w["conf_conv_b"][:, ls], (CONV_ROWS, LANES))
            for k in range(CONF_KERNEL):
                off = GLU_HALO - (CONF_KERNEL - 1) + k
                r = off % SUBLANES
                src = glu_ext_ref.at[lt] if r == 0 else shift_ref.at[r - 1, lt]
                acc = acc + src[pl.ds(off - r + rb * CONV_ROWS, CONV_ROWS), :] * w["conf_conv_w"][k:k + 1, ls]
            hc_ref[pl.ds(rb * CONV_ROWS, CONV_ROWS), ls] = acc
    glu_ext_ref[:, 0:GLU_HALO, :] = glu_ext_ref[:, L:L + GLU_HALO, :]
    hc = hc_ref[...]
    mu = jnp.mean(hc, axis=-1, keepdims=True)
    hcc = hc - mu
    var = jnp.mean(hcc * hcc, axis=-1, keepdims=True)
    hn = hcc * lax.rsqrt(var + EPS) * w["conf_ln_g"][...] + w["conf_ln_b"][...]
    y_conf = _silu(hn)

    mix = jnp.concatenate([y_ssd, y_conf], axis=1).astype(BF16)
    return x + _dot(mix, w["w_out"][...])


_MIX_WEIGHTS = ("norm_mix_w", "w_in", "ssd_conv_w", "ssd_conv_b", "ssd_dt_bias", "ssd_a_log",
                "ssd_d", "ssd_norm_w", "conf_conv_w", "conf_conv_b", "conf_ln_g", "conf_ln_b", "w_out")


def _meta_kernel(x_ref, keep_ref, *refs):
    nw = len(_MIX_WEIGHTS)
    w = dict(zip(_MIX_WEIGHTS, refs[:nw]))
    state_ref, xbc_halo_ref, glu_halo_ref, xbc_ext_ref, glu_ext_ref = refs[nw:]
    state_ref[...] = jnp.zeros(state_ref.shape, F32)
    xbc_ext_ref[0:XBC_HALO, :] = jnp.zeros((XBC_HALO, xbc_ext_ref.shape[1]), F32)
    glu_ext_ref[:, 0:GLU_HALO, :] = jnp.zeros(glu_halo_ref.shape, F32)
    _mixer_tile(x_ref[...], keep_ref[...], w, state_ref, xbc_ext_ref, glu_ext_ref)
    xbc_halo_ref[...] = xbc_ext_ref[0:XBC_HALO, :]
    glu_halo_ref[...] = glu_ext_ref[:, 0:GLU_HALO, :]


def _mixer_kernel(x_ref, state0_ref, xbc_halo0_ref, glu_halo0_ref, *refs):
    nw = len(_MIX_WEIGHTS)
    w = dict(zip(_MIX_WEIGHTS, refs[:nw]))
    out_ref, state_ref, xbc_ext_ref, glu_ext_ref, shift_ref, hc_ref = refs[nw:]

    @pl.when(pl.program_id(1) == 0)
    def _():
        state_ref[...] = state0_ref[...]
        xbc_ext_ref[0:XBC_HALO, :] = xbc_halo0_ref[...]
        glu_ext_ref[:, 0:GLU_HALO, :] = glu_halo0_ref[...]

    out_ref[...] = _mixer_tile(x_ref[...], None, w, state_ref, xbc_ext_ref, glu_ext_ref,
                               (shift_ref, hc_ref))


def _const_spec(shape):
    nd = len(shape)
    return pl.BlockSpec(shape, lambda *_: (0,) * nd, pipeline_mode=pl.Buffered(1))


def _mixer_calls(x, meta_tile, keep, wts):
    b, seq, d = x.shape
    n_heads = d // SSD_HEADDIM
    gw = (n_heads // SSD_GROUPS) * SSD_HEADDIM
    d_xbc = d + 2 * SSD_GROUPS * SSD_STATE
    w_list = [wts[k] for k in _MIX_WEIGHTS]
    state_shape = (SSD_GROUPS, SSD_STATE, gw)
    glu_halo_shape = (d // LANES, GLU_HALO, LANES)

    lm = meta_tile.shape[0]
    state0, xbc_halo0, glu_halo0 = pl.pallas_call(
        _meta_kernel,
        out_shape=(jax.ShapeDtypeStruct(state_shape, F32),
                   jax.ShapeDtypeStruct((XBC_HALO, d_xbc), F32),
                   jax.ShapeDtypeStruct(glu_halo_shape, F32)),
        scratch_shapes=[pltpu.VMEM((XBC_HALO + lm, d_xbc), F32),
                        pltpu.VMEM((d // LANES, GLU_HALO + lm, LANES), F32)],
        compiler_params=pltpu.CompilerParams(vmem_limit_bytes=VMEM_LIMIT_BYTES),
        name="meta_prologue",
    )(meta_tile, keep, *w_list)

    lt = MIX_TILE
    assert seq % lt == 0
    return pl.pallas_call(
        _mixer_kernel,
        grid=(b, seq // lt),
        in_specs=[pl.BlockSpec((None, lt, d), lambda i, c: (i, c, 0)),
                  _const_spec(state_shape), _const_spec((XBC_HALO, d_xbc)), _const_spec(glu_halo_shape)]
                 + [_const_spec(a.shape) for a in w_list],
        out_specs=pl.BlockSpec((None, lt, d), lambda i, c: (i, c, 0)),
        out_shape=jax.ShapeDtypeStruct((b, seq, d), F32),
        scratch_shapes=[pltpu.VMEM(state_shape, F32),
                        pltpu.VMEM((XBC_HALO + lt, d_xbc), F32),
                        pltpu.VMEM((d // LANES, GLU_HALO + lt, LANES), F32),
                        pltpu.VMEM((SUBLANES - 1, d // LANES, GLU_HALO + lt - SUBLANES, LANES), F32),
                        pltpu.VMEM((lt, d), F32)],
        compiler_params=pltpu.CompilerParams(
            dimension_semantics=("arbitrary", "arbitrary"), vmem_limit_bytes=VMEM_LIMIT_BYTES),
        name="mixer",
    )(x, state0, xbc_halo0, glu_halo0, *w_list)


def _sort_desc(v):
    v = list(v)
    n = len(v)
    k = 2
    while k <= n:
        j = k // 2
        while j >= 1:
            for i in range(n):
                l = i ^ j
                if l > i:
                    hi, lo = jnp.maximum(v[i], v[l]), jnp.minimum(v[i], v[l])
                    v[i], v[l] = (hi, lo) if (i & k) == 0 else (lo, hi)
            j //= 2
        k *= 2
    return v


def _merge_top(a, b, *, sort=True):
    n = len(a)
    v = [jnp.maximum(a[i], b[n - 1 - i]) for i in range(n)]
    if not sort:
        return v
    j = n // 2
    while j >= 1:
        for i in range(n):
            l = i ^ j
            if l > i:
                v[i], v[l] = jnp.maximum(v[i], v[l]), jnp.minimum(v[i], v[l])
        j //= 2
    return v


def _top_sorted(s):
    n = s.shape[0] // SUBLANES
    v = _sort_desc([s[SUBLANES * i:SUBLANES * (i + 1), :] for i in range(n)])
    for shift in (4, 2, 1):
        v = _merge_top(v, [pltpu.roll(a, shift, axis=0) for a in v])
    return v


def _top_pair_sums(v1, v2):
    t = v1[0].shape[1]
    sub = lax.broadcasted_iota(jnp.int32, (SUBLANES, t), 0)
    cols = []
    for q in range(PEER_TOPK // SUBLANES):
        col = v1[SUBLANES * q]
        for s in range(1, SUBLANES):
            col = jnp.where(sub == s, v1[SUBLANES * q + s], col)
        cols.append(col)
    m = [cols[0] + b for b in v2]
    m[-1] = jnp.maximum(m[-1], cols[1] + v2[0])
    for i in range(PEER_TOPK - 2, -1, -1):
        m[i], m[i + 1] = jnp.maximum(m[i], m[i + 1]), jnp.minimum(m[i], m[i + 1])
    m = _merge_top(m, [pltpu.roll(a, 4, axis=0) for a in m])
    m = _merge_top(m, [pltpu.roll(a, 2, axis=0) for a in m])
    return _merge_top(m, [pltpu.roll(a, 1, axis=0) for a in m], sort=False)


def _peer_scores(xt_ref, wq_ref, k1_ref, k2_ref, s1_ref, s2_ref, tau_ref):
    tt = xt_ref.shape[1]
    dk = k1_ref.shape[2]
    qt = _dot(wq_ref[...], xt_ref[...]).astype(BF16)
    for h in range(PEER_HEADS):
        s1 = _dot(k1_ref[h], qt[2 * dk * h:2 * dk * h + dk, :]) * LOG2E
        s2 = _dot(k2_ref[h], qt[2 * dk * h + dk:2 * dk * (h + 1), :]) * LOG2E
        for lt in range(tt // LANES):
            ls = slice(lt * LANES, (lt + 1) * LANES)
            v1 = _top_sorted(s1[:, ls])
            v2 = _top_sorted(s2[:, ls])
            top = v1[0] + v2[0]
            z = functools.reduce(lambda a, b: a + b, [jnp.exp2(x - top) for x in _top_pair_sums(v1, v2)])
            c = jnp.broadcast_to((top + jnp.log2(z) + 1.0)[0:1, :], top.shape)
            tau_ref[h, :, ls] = functools.reduce(jnp.minimum, _top_pair_sums([v - c for v in v1], v2))
            s1y = s1[:, ls] - c[0:1, :]
            for blk in range(PEER_NKEYS // SUBLANES):
                s1_ref[h, blk, :, ls] = s1y[blk * SUBLANES:(blk + 1) * SUBLANES, :]
            s2_ref[h, lt] = s2[:, ls]


def _peer_activations(lt, chunk, s_ref, a_ref, s1_ref, s2_ref, tau_ref):
    ls = pl.ds(lt * LANES, LANES)
    bshape = (GATE_ROWS, LANES)
    tau = [jnp.broadcast_to(tau_ref[h, 0:1, ls], bshape) for h in range(PEER_HEADS)]
    for k in range(SUBLANES):
        s1b = [jnp.broadcast_to(s1_ref[h, chunk, k:k + 1, ls], bshape) for h in range(PEER_HEADS)]
        for blk in range(PEER_NKEYS // GATE_ROWS):
            rs = pl.ds(blk * GATE_ROWS, GATE_ROWS)
            gate = None
            for h in range(PEER_HEADS):
                y = s1b[h] + s2_ref[h, lt, rs, :]
                term = jnp.where(y >= tau[h], jnp.exp2(y), 0.0)
                gate = term if gate is None else gate + term
            rows = pl.ds(k * PEER_NKEYS + blk * GATE_ROWS, GATE_ROWS)
            pre = s_ref[rows, ls]
            a_ref[rows, ls] = (pre * (1.0 + lax.erf(pre * math.sqrt(0.5))) * gate).astype(BF16)


def _peer_kernel(hcur_ref, hres_ref, nfw_ref, wq_ref, k1_ref, k2_ref, wd_ref, wu_ref, nlw_ref, out_ref,
                 xt_ref, acc_ref, sa_ref, sb_ref, aa_ref, ab_ref, s1_ref, s2_ref, tau_ref,
                 *, pairs_per_tile, n_pairs):
    g = pl.program_id(0)
    tt, d = hcur_ref.shape
    ec = wd_ref.shape[0] // 2
    assert ec == SUBLANES * PEER_NKEYS
    tile_start = g % pairs_per_tile == 0
    in_range = g < n_pairs

    @pl.when(g == 0)
    def _():
        for ref in (sb_ref, aa_ref, acc_ref, s1_ref, s2_ref, tau_ref):
            ref[...] = jnp.zeros(ref.shape, ref.dtype)

    @pl.when(jnp.logical_and(tile_start, in_range))
    def _():
        xn = _rmsnorm(hcur_ref[...], nfw_ref[...])
        xt_ref[...] = xn.T.astype(BF16)

    def step(sub, s_new, s_old, a_new, a_old, chunk_old):
        wd = wd_ref.at[pl.ds(sub * ec, ec), :]
        wu = wu_ref.at[:, pl.ds(sub * ec, ec)]
        s_new[...] = _dot(wd[...], xt_ref[...])
        acc_ref[...] += _dot(wu[...], a_old[...])
        for lt in range(tt // LANES):
            _peer_activations(lt, chunk_old, s_old, a_new, s1_ref, s2_ref, tau_ref)

    chunks = 2 * pairs_per_tile
    step(0, sa_ref, sb_ref, ab_ref, aa_ref, jnp.maximum(2 * g - 1, 0) % chunks)

    @pl.when(jnp.logical_and(tile_start, in_range))
    def _():
        _peer_scores(xt_ref, wq_ref, k1_ref, k2_ref, s1_ref, s2_ref, tau_ref)

    step(1, sb_ref, sa_ref, aa_ref, ab_ref, (2 * g) % chunks)

    @pl.when(jnp.logical_and(tile_start, g > 0))
    def _():
        out_ref[...] = _rmsnorm(hres_ref[...] + acc_ref[...].T, nlw_ref[...])
        acc_ref[...] = jnp.zeros(acc_ref.shape, F32)


def _peer_call(h, nfw, wq_t, k1, k2, wd, wu_t, nlw):
    t, d = h.shape
    n_exp = wd.shape[0]
    tt, ec = PEER_TOK, PEER_EXP
    assert t % tt == 0 and n_exp % (2 * ec) == 0
    pairs_per_tile = n_exp // (2 * ec)
    n_pairs = (t // tt) * pairs_per_tile
    cur_pair = lambda g: jnp.minimum(g, n_pairs - 1)
    done_tile = lambda g: jnp.maximum(2 * g - 1, 0) // (2 * pairs_per_tile)
    gate_shape = (PEER_HEADS, tt // LANES, PEER_NKEYS, LANES)
    row_shape = (PEER_HEADS, PEER_NKEYS // SUBLANES, SUBLANES, tt)
    return pl.pallas_call(
        functools.partial(_peer_kernel, pairs_per_tile=pairs_per_tile, n_pairs=n_pairs),
        grid=(n_pairs + 1,),
        in_specs=[pl.BlockSpec((tt, d), lambda g: (cur_pair(g) // pairs_per_tile, 0)),
                  pl.BlockSpec((tt, d), lambda g: (done_tile(g), 0)),
                  _const_spec(nfw.shape), _const_spec(wq_t.shape), _const_spec(k1.shape),
                  _const_spec(k2.shape),
                  pl.BlockSpec((2 * ec, d), lambda g: (cur_pair(g) % pairs_per_tile, 0)),
                  pl.BlockSpec((d, 2 * ec), lambda g: (0, jnp.maximum(g - 1, 0) % pairs_per_tile)),
                  _const_spec(nlw.shape)],
        out_specs=pl.BlockSpec((tt, d), lambda g: (done_tile(g), 0)),
        out_shape=jax.ShapeDtypeStruct((t, d), F32),
        scratch_shapes=[pltpu.VMEM((d, tt), BF16),
                        pltpu.VMEM((d, tt), F32),
                        pltpu.VMEM((ec, tt), F32), pltpu.VMEM((ec, tt), F32),
                        pltpu.VMEM((ec, tt), BF16), pltpu.VMEM((ec, tt), BF16),
                        pltpu.VMEM(row_shape, F32), pltpu.VMEM(gate_shape, F32),
                        pltpu.VMEM((PEER_HEADS, SUBLANES, tt), F32)],
        compiler_params=pltpu.CompilerParams(
            dimension_semantics=("arbitrary",), vmem_limit_bytes=VMEM_LIMIT_BYTES),
        name="peer",
    )(h, h, nfw, wq_t, k1, k2, wd, wu_t, nlw)


def _pad_lanes(a, width):
    return jnp.pad(a, [(0, 0)] * (a.ndim - 1) + [(0, width - a.shape[-1])])


def kernel(x, meta_tokens, norm_mix_w, w_in, ssd_conv_w, ssd_conv_b, ssd_dt_bias, ssd_A_log, ssd_D,
           ssd_norm_w, conf_conv_w, conf_conv_b, conf_ln_g, conf_ln_b, w_out, norm_ffn_w,
           peer_w_query, peer_sub_keys_1, peer_sub_keys_2, peer_w_down, peer_w_up, norm_final_w):
    b, seq, d = x.shape
    assert norm_mix_w.shape[0] == 1, "one layer"
    d_xbc = d + 2 * SSD_GROUPS * SSD_STATE
    n_heads = d // SSD_HEADDIM
    row = lambda a: a.reshape(1, -1).astype(F32)

    wi = w_in[0]
    o1, o2, o3 = d, d + d_xbc, d + d_xbc + n_heads
    w_in_r = jnp.concatenate([wi[:, :o1], wi[:, o1:o2], wi[:, o3:], _pad_lanes(wi[:, o2:o3], DT_PAD)],
                             axis=1).astype(BF16)
    wts = {
        "norm_mix_w": row(norm_mix_w[0]), "w_in": w_in_r,
        "ssd_conv_w": ssd_conv_w[0].astype(F32), "ssd_conv_b": row(ssd_conv_b[0]),
        "ssd_dt_bias": _pad_lanes(row(ssd_dt_bias[0]), DT_PAD),
        "ssd_a_log": _pad_lanes(row(ssd_A_log[0]), DT_PAD),
        "ssd_d": row(jnp.repeat(ssd_D[0], SSD_HEADDIM)), "ssd_norm_w": row(ssd_norm_w[0]),
        "conf_conv_w": conf_conv_w[0].astype(F32), "conf_conv_b": row(conf_conv_b[0]),
        "conf_ln_g": row(conf_ln_g[0]), "conf_ln_b": row(conf_ln_b[0]),
        "w_out": w_out[0].astype(BF16),
    }

    pad = SSD_SUB - N_META
    meta_tile = jnp.pad(meta_tokens.astype(F32), [(pad, 0), (0, 0)])
    keep = jnp.pad(jnp.ones((N_META, DT_PAD), F32), [(pad, 0), (0, 0)])
    h1 = _mixer_calls(x.astype(F32), meta_tile, keep, wts)

    out = _peer_call(
        h1.reshape(b * seq, d), row(norm_ffn_w[0]),
        peer_w_query[0].T.astype(BF16), peer_sub_keys_1[0].astype(BF16), peer_sub_keys_2[0].astype(BF16),
        peer_w_down[0].astype(BF16), peer_w_up[0].T.astype(BF16), row(norm_final_w))
    return out.reshape(b, seq, d).astype(x.dtype)
```
